```python
import jax, jax.numpy as jnp
from jax import lax
import numpy as np

D_MODEL = 1024
BATCH = 8
SEQ = 4096
DEPTH = 1

EPS = 1e-6
M_HEADS = 4
M_HEAD_DIM = D_MODEL // M_HEADS
M_WIDTH = M_HEADS * M_HEAD_DIM
M_CONV = 4
M_CHUNK = 64
N_Q_HEADS = 8
N_KV_GROUPS = 2
N_HEADS_PER_GROUP = N_Q_HEADS // N_KV_GROUPS
N_HEAD_DIM = D_MODEL // N_Q_HEADS
N_WIDTH = N_Q_HEADS * N_HEAD_DIM
N_KV_WIDTH = N_KV_GROUPS * N_HEAD_DIM
CMP_BLOCK = 32
CMP_STRIDE = 16
SLC_BLOCK = 64
SLC_TOP = 8
SLC_Q_CHUNK = 32
WINDOW = 512
Q_BLOCK = 128
ROPE_THETA = 10000.0
P_HEADS = 8
P_KEYS = 128
P_EXPERTS = P_KEYS * P_KEYS
P_QUERY_DIM = 128
P_HALF = P_QUERY_DIM // 2
P_TOPK = 16
P_CHUNK = 128

IN_WIDTHS = (2 * M_WIDTH, M_WIDTH, M_WIDTH, M_HEADS, M_HEADS,
             N_WIDTH, N_KV_WIDTH, N_KV_WIDTH, N_KV_WIDTH, N_KV_WIDTH, N_KV_WIDTH, N_KV_WIDTH,
             3 * N_Q_HEADS, D_MODEL, D_MODEL)
IN_TOTAL = sum(IN_WIDTHS)

kernel_name = 'hybrid_mlstm_nsa_peer_block'


def rmsnorm(x, g):
    xf = x.astype(jnp.float32)
    y = xf * lax.rsqrt(jnp.mean(xf * xf, axis=-1, keepdims=True) + EPS)
    return (y * g.astype(jnp.float32)).astype(x.dtype)


def masked_softmax(s, mask):
    s = jnp.where(mask, s, -1e30)
    return jnp.where(mask, jax.nn.softmax(s, axis=-1), 0.0)


def split_cols(z, widths):
    outs, off = [], 0
    for w in widths:
        outs.append(z[..., off:off + w])
        off += w
    return outs


def rope_tables(seq, dim):
    inv = ROPE_THETA ** (-jnp.arange(0, dim, 2, dtype=jnp.float32) / dim)
    ang = jnp.arange(seq, dtype=jnp.float32)[:, None] * inv[None, :]
    ang = jnp.concatenate([ang, ang], axis=-1)
    return jnp.cos(ang), jnp.sin(ang)


def apply_rope(x, cos, sin):
    xf = x.astype(jnp.float32)
    x1, x2 = jnp.split(xf, 2, axis=-1)
    rot = jnp.concatenate([-x2, x1], axis=-1)
    return (xf * cos[None, :, None, :] + rot * sin[None, :, None, :]).astype(x.dtype)


def causal_dwconv(x, w, b):
    y = lax.conv_general_dilated(x, w[:, None, :].astype(x.dtype), window_strides=(1,),
                                 padding=[(M_CONV - 1, 0)],
                                 dimension_numbers=('NWC', 'WIO', 'NWC'),
                                 feature_group_count=x.shape[-1])
    return y + b.astype(y.dtype)


def mlstm_chunkwise(q, k, v, i_pre, f_pre):
    B, S, H, d = q.shape
    L = M_CHUNK
    nc = S // L

    def to_chunks(a):
        return a.astype(jnp.float32).reshape(B, nc, L, H, -1).transpose(1, 0, 3, 2, 4)

    qc, kc, vc = to_chunks(q), to_chunks(k), to_chunks(v)
    lf = jax.nn.log_sigmoid(f_pre.astype(jnp.float32)).reshape(B, nc, L, H).transpose(1, 0, 3, 2)
    li = i_pre.astype(jnp.float32).reshape(B, nc, L, H).transpose(1, 0, 3, 2)
    causal = jnp.tril(jnp.ones((L, L), dtype=bool))

    def step(carry, inp):
        C, n, m = carry
        qb, kb, vb, lfb, lib = inp
        b = jnp.cumsum(lfb, axis=-1)
        Dm = jnp.where(causal, b[..., :, None] - b[..., None, :] + lib[..., None, :], -jnp.inf)
        inter = b + m[..., None]
        mt = jnp.maximum(inter, jnp.max(Dm, axis=-1))
        Dw = jnp.exp(Dm - mt[..., None])
        iw = jnp.exp(inter - mt)
        W = jnp.einsum('bhld,bhsd->bhls', qb, kb) * Dw
        num = iw[..., None] * jnp.einsum('bhvk,bhlk->bhlv', C, qb) + jnp.einsum('bhls,bhsv->bhlv', W, vb)
        den = iw * jnp.einsum('bhk,bhlk->bhl', n, qb) + jnp.sum(W, axis=-1)
        h = num / jnp.maximum(jnp.abs(den), jnp.exp(-mt))[..., None]
        bL = b[..., -1]
        g = bL[..., None] - b + lib
        m_new = jnp.maximum(bL + m, jnp.max(g, axis=-1))
        a = jnp.exp(bL + m - m_new)
        w = jnp.exp(g - m_new[..., None])
        C_new = a[..., None, None] * C + jnp.einsum('bhlv,bhlk->bhvk', vb * w[..., None], kb)
        n_new = a[..., None] * n + jnp.einsum('bhl,bhlk->bhk', w, kb)
        return (C_new, n_new, m_new), h

    init = (jnp.zeros((B, H, d, d), jnp.float32), jnp.zeros((B, H, d), jnp.float32),
            jnp.zeros((B, H), jnp.float32))
    _, hs = lax.scan(step, init, (qc, kc, vc, lf, li))
    return hs.transpose(1, 0, 3, 2, 4).reshape(B, S, H, d)


def mlstm_branch(m_qk, m_v, m_o, m_i, m_f, conv_w, conv_b, i_bias, f_bias, norm_g):
    B, S, _ = m_v.shape
    qk = jax.nn.silu(causal_dwconv(m_qk, conv_w, conv_b))
    q, k = jnp.split(qk, 2, axis=-1)
    q = q.reshape(B, S, M_HEADS, M_HEAD_DIM)
    k = k.reshape(B, S, M_HEADS, M_HEAD_DIM) * (M_HEAD_DIM ** -0.5)
    v = m_v.reshape(B, S, M_HEADS, M_HEAD_DIM)
    i_pre = m_i + i_bias.astype(m_i.dtype)
    f_pre = m_f + f_bias.astype(m_f.dtype)
    h = mlstm_chunkwise(q, k, v, i_pre, f_pre)
    h = h * lax.rsqrt(jnp.mean(h * h, axis=-1, keepdims=True) + EPS)
    h = h.reshape(B, S, M_WIDTH) * norm_g.astype(jnp.float32) * jax.nn.sigmoid(m_o.astype(jnp.float32))
    return h.astype(m_v.dtype)


def compress(k, pos, w1, w2):
    B, S, G, d = k.shape
    n = (S - CMP_BLOCK) // CMP_STRIDE + 1
    idx = jnp.arange(n)[:, None] * CMP_STRIDE + jnp.arange(CMP_BLOCK)[None, :]
    kb = k[:, idx] + pos.astype(k.dtype)[None, None, :, None, :]
    kb = kb.transpose(0, 1, 3, 2, 4).reshape(B, n, G, CMP_BLOCK * d)
    return jax.nn.gelu(kb @ w1, approximate=False) @ w2


def selected_attention(q_r, k, v, top_idx, blk_ok):
    B, S, _, d = q_r.shape
    n_top = top_idx.shape[-1]
    nq = S // SLC_Q_CHUNK
    n_keys = n_top * SLC_BLOCK
    scale = N_HEAD_DIM ** -0.5
    qc = q_r.reshape(B, nq, SLC_Q_CHUNK, N_KV_GROUPS, N_HEADS_PER_GROUP, d).transpose(1, 0, 2, 3, 4, 5)
    ic = top_idx.reshape(B, N_KV_GROUPS, nq, SLC_Q_CHUNK, n_top).transpose(2, 0, 1, 3, 4)
    oc = blk_ok.reshape(B, N_KV_GROUPS, nq, SLC_Q_CHUNK, n_top).transpose(2, 0, 1, 3, 4)
    tc = jnp.arange(S).reshape(nq, SLC_Q_CHUNK)
    k_bg = k.transpose(0, 2, 1, 3)
    v_bg = v.transpose(0, 2, 1, 3)
    gather = jax.vmap(jax.vmap(lambda arr, ii: arr[ii]))

    def one(args):
        qb, ib, ob, tb = args
        kpos = ib[..., None] * SLC_BLOCK + jnp.arange(SLC_BLOCK)
        flat = kpos.reshape(B, N_KV_GROUPS, SLC_Q_CHUNK * n_keys)
        kg = gather(k_bg, flat).reshape(B, N_KV_GROUPS, SLC_Q_CHUNK, n_keys, d)
        vg = gather(v_bg, flat).reshape(B, N_KV_GROUPS, SLC_Q_CHUNK, n_keys, d)
        mask = ((kpos <= tb[None, None, :, None, None]) & ob[..., None]).reshape(
            B, N_KV_GROUPS, 1, SLC_Q_CHUNK, n_keys)
        s = jnp.einsum('bqghd,bgqkd->bghqk', qb, kg).astype(jnp.float32) * scale
        p = masked_softmax(s, mask)
        return jnp.einsum('bghqk,bgqkd->bqghd', p.astype(qb.dtype), vg)

    out = lax.map(one, (qc, ic, oc, tc))
    return out.transpose(1, 0, 2, 3, 4, 5).reshape(B, S, N_Q_HEADS, d)


def window_attention(q_r, k, v):
    B, S, _, d = q_r.shape
    nb = S // Q_BLOCK
    span = WINDOW + Q_BLOCK
    scale = N_HEAD_DIM ** -0.5
    qb = q_r.reshape(B, nb, Q_BLOCK, N_KV_GROUPS, N_HEADS_PER_GROUP, d).transpose(1, 0, 2, 3, 4, 5)
    kpad = jnp.pad(k, ((0, 0), (WINDOW, 0), (0, 0), (0, 0)))
    vpad = jnp.pad(v, ((0, 0), (WINDOW, 0), (0, 0), (0, 0)))

    def one(args):
        qq, bi = args
        start = bi * Q_BLOCK
        kk = lax.dynamic_slice_in_dim(kpad, start, span, axis=1)
        vv = lax.dynamic_slice_in_dim(vpad, start, span, axis=1)
        tq = start + jnp.arange(Q_BLOCK)
        kp = start - WINDOW + jnp.arange(span)
        mask = (kp[None, :] <= tq[:, None]) & (kp[None, :] > tq[:, None] - WINDOW) & (kp[None, :] >= 0)
        s = jnp.einsum('bqghd,bkgd->bghqk', qq, kk).astype(jnp.float32) * scale
        p = masked_softmax(s, mask)
        return jnp.einsum('bghqk,bkgd->bqghd', p.astype(qq.dtype), vv)

    out = lax.map(one, (qb, jnp.arange(nb)))
    return out.transpose(1, 0, 2, 3, 4, 5).reshape(B, S, N_Q_HEADS, d)


def nsa_branch(n_q, n_kc, n_vc, n_ks, n_vs, n_kw, n_vw, n_g,
               cmp_k_pos, cmp_k_w1, cmp_k_w2, cmp_v_pos, cmp_v_w1, cmp_v_w2):
    B, S, _ = n_q.shape
    dt = n_q.dtype
    d = N_HEAD_DIM
    scale = d ** -0.5
    q = n_q.reshape(B, S, N_Q_HEADS, d)
    kv = lambda a: a.reshape(B, S, N_KV_GROUPS, d)
    kc, vc, ks, vs, kw, vw = kv(n_kc), kv(n_vc), kv(n_ks), kv(n_vs), kv(n_kw), kv(n_vw)
    cos, sin = rope_tables(S, d)
    q_r = apply_rope(q, cos, sin)
    ks_r = apply_rope(ks, cos, sin)
    kw_r = apply_rope(kw, cos, sin)
    t = jnp.arange(S)
    kcmp = compress(kc, cmp_k_pos, cmp_k_w1, cmp_k_w2)
    vcmp = compress(vc, cmp_v_pos, cmp_v_w1, cmp_v_w2)
    n_cmp = kcmp.shape[1]
    cmp_start = jnp.arange(n_cmp) * CMP_STRIDE
    cmask = (cmp_start + CMP_BLOCK - 1)[None, :] <= t[:, None]
    qg = q.reshape(B, S, N_KV_GROUPS, N_HEADS_PER_GROUP, d)
    s = jnp.einsum('bsghd,bngd->bghsn', qg, kcmp).astype(jnp.float32) * scale
    p_cmp = masked_softmax(s, cmask)
    o_cmp = jnp.einsum('bghsn,bngd->bsghd', p_cmp.astype(dt), vcmp).reshape(B, S, N_Q_HEADS, d)
    n_slc = S // SLC_BLOCK
    slc_start = jnp.arange(n_slc) * SLC_BLOCK
    overlap = ((cmp_start[:, None] < slc_start[None, :] + SLC_BLOCK) &
               (cmp_start[:, None] + CMP_BLOCK > slc_start[None, :])).astype(jnp.float32)
    imp = jnp.einsum('bghsn,nj->bgsj', p_cmp, overlap)
    cur = t // SLC_BLOCK
    j = jnp.arange(n_slc)
    forced = (j[None, :] == 0) | (j[None, :] == cur[:, None]) | (j[None, :] == cur[:, None] - 1)
    valid = slc_start[None, :] <= t[:, None]
    score = jnp.where(forced, 1e9, jnp.where(valid, imp, -1e9))
    top_vals, top_idx = lax.top_k(score, min(SLC_TOP, n_slc))
    blk_ok = top_vals > -1e8
    o_slc = selected_attention(q_r, ks_r, vs, top_idx, blk_ok)
    o_win = window_attention(q_r, kw_r, vw)
    g = jax.nn.sigmoid(n_g.astype(jnp.float32)).reshape(B, S, N_Q_HEADS, 3)
    o = (g[..., 0:1] * o_cmp.astype(jnp.float32) + g[..., 1:2] * o_slc.astype(jnp.float32)
         + g[..., 2:3] * o_win.astype(jnp.float32))
    return o.reshape(B, S, N_WIDTH).astype(dt)


def token_mixer(xn, w_in, m_conv_w, m_conv_b, m_i_bias, m_f_bias, m_norm_g, w_m_out,
                cmp_k_pos, cmp_k_w1, cmp_k_w2, cmp_v_pos, cmp_v_w1, cmp_v_w2, w_n_out, w_out):
    z = xn @ w_in
    (m_qk, m_v, m_o, m_i, m_f, n_q, n_kc, n_vc, n_ks, n_vs, n_kw, n_vw,
     n_g, g_a, g_b) = split_cols(z, IN_WIDTHS)
    y_a = mlstm_branch(m_qk, m_v, m_o, m_i, m_f, m_conv_w, m_conv_b, m_i_bias, m_f_bias, m_norm_g) @ w_m_out
    y_b = nsa_branch(n_q, n_kc, n_vc, n_ks, n_vs, n_kw, n_vw, n_g,
                     cmp_k_pos, cmp_k_w1, cmp_k_w2, cmp_v_pos, cmp_v_w1, cmp_v_w2) @ w_n_out
    mix = (jax.nn.sigmoid(g_a.astype(jnp.float32)) * y_a.astype(jnp.float32)
           + jax.nn.sigmoid(g_b.astype(jnp.float32)) * y_b.astype(jnp.float32))
    return mix.astype(xn.dtype) @ w_out


def peer(xn, wq, k1, k2, u, v):
    B, S, D = xn.shape
    xs = xn.reshape((B * S) // P_CHUNK, P_CHUNK, D)

    def one(xt):
        q = (xt @ wq).reshape(P_CHUNK, P_HEADS, P_QUERY_DIM)
        q1, q2 = q[..., :P_HALF], q[..., P_HALF:]
        s1 = jnp.einsum('thd,nd->thn', q1, k1).astype(jnp.float32)
        s2 = jnp.einsum('thd,nd->thn', q2, k2).astype(jnp.float32)
        v1, i1 = lax.top_k(s1, P_TOPK)
        v2, i2 = lax.top_k(s2, P_TOPK)
        cand = (v1[..., :, None] + v2[..., None, :]).reshape(P_CHUNK, P_HEADS, P_TOPK * P_TOPK)
        sv, ci = lax.top_k(cand, P_TOPK)
        e = (jnp.take_along_axis(i1, ci // P_TOPK, axis=-1) * P_KEYS
             + jnp.take_along_axis(i2, ci % P_TOPK, axis=-1))
        gate = jax.nn.softmax(sv, axis=-1)
        act = jax.nn.gelu(jnp.einsum('td,thkd->thk', xt, u[e]).astype(jnp.float32), approximate=False)
        return jnp.einsum('thk,thkd->td', (gate * act).astype(xt.dtype), v[e])

    return lax.map(one, xs).reshape(B, S, D)


def setup_inputs(seed: int = 0) -> dict:
    key = jax.random.key(seed)
    ks = jax.random.split(key, 24)
    L = DEPTH

    def nrm(k, shape, scale):
        return jax.random.normal(k, shape, jnp.float32) * scale

    return {
        'x': nrm(ks[0], (BATCH, SEQ, D_MODEL), 1.0),
        'ln_mix_g': 1.0 + nrm(ks[1], (L, D_MODEL), 0.01),
        'w_in': nrm(ks[2], (L, D_MODEL, IN_TOTAL), D_MODEL ** -0.5),
        'm_conv_w': nrm(ks[3], (L, M_CONV, 2 * M_WIDTH), M_CONV ** -0.5),
        'm_conv_b': nrm(ks[4], (L, 2 * M_WIDTH), 0.01),
        'm_i_bias': nrm(ks[5], (L, M_HEADS), 0.1),
        'm_f_bias': jnp.linspace(3.0, 6.0, M_HEADS, dtype=jnp.float32)[None, :] + nrm(ks[6], (L, M_HEADS), 0.1),
        'm_norm_g': 1.0 + nrm(ks[7], (L, M_WIDTH), 0.01),
        'w_m_out': nrm(ks[8], (L, M_WIDTH, D_MODEL), M_WIDTH ** -0.5),
        'cmp_k_pos': nrm(ks[9], (L, CMP_BLOCK, N_HEAD_DIM), 0.1),
        'cmp_k_w1': nrm(ks[10], (L, CMP_BLOCK * N_HEAD_DIM, N_HEAD_DIM), (CMP_BLOCK * N_HEAD_DIM) ** -0.5),
        'cmp_k_w2': nrm(ks[11], (L, N_HEAD_DIM, N_HEAD_DIM), N_HEAD_DIM ** -0.5),
        'cmp_v_pos': nrm(ks[12], (L, CMP_BLOCK, N_HEAD_DIM), 0.1),
        'cmp_v_w1': nrm(ks[13], (L, CMP_BLOCK * N_HEAD_DIM, N_HEAD_DIM), (CMP_BLOCK * N_HEAD_DIM) ** -0.5),
        'cmp_v_w2': nrm(ks[14], (L, N_HEAD_DIM, N_HEAD_DIM), N_HEAD_DIM ** -0.5),
        'w_n_out': nrm(ks[15], (L, N_WIDTH, D_MODEL), N_WIDTH ** -0.5),
        'w_out': nrm(ks[16], (L, D_MODEL, D_MODEL), D_MODEL ** -0.5),
        'ln_ffn_g': 1.0 + nrm(ks[17], (L, D_MODEL), 0.01),
        'peer_wq': nrm(ks[18], (L, D_MODEL, P_HEADS * P_QUERY_DIM), D_MODEL ** -0.5),
        'peer_k1': nrm(ks[19], (L, P_KEYS, P_HALF), P_HALF ** -0.5),
        'peer_k2': nrm(ks[20], (L, P_KEYS, P_HALF), P_HALF ** -0.5),
        'peer_u': nrm(ks[21], (L, P_EXPERTS, D_MODEL), D_MODEL ** -0.5),
        'peer_v': nrm(ks[22], (L, P_EXPERTS, D_MODEL), P_HEADS ** -0.5),
        'ln_f_g': 1.0 + nrm(ks[23], (D_MODEL,), 0.01),
    }


def reference(x, ln_mix_g, w_in, m_conv_w, m_conv_b, m_i_bias, m_f_bias, m_norm_g, w_m_out,
              cmp_k_pos, cmp_k_w1, cmp_k_w2, cmp_v_pos, cmp_v_w1, cmp_v_w2, w_n_out, w_out,
              ln_ffn_g, peer_wq, peer_k1, peer_k2, peer_u, peer_v, ln_f_g):
    h = x
    for l in range(DEPTH):
        h = h + token_mixer(rmsnorm(h, ln_mix_g[l]), w_in[l], m_conv_w[l], m_conv_b[l], m_i_bias[l],
                            m_f_bias[l], m_norm_g[l], w_m_out[l], cmp_k_pos[l], cmp_k_w1[l], cmp_k_w2[l],
                            cmp_v_pos[l], cmp_v_w1[l], cmp_v_w2[l], w_n_out[l], w_out[l])
        h = h + peer(rmsnorm(h, ln_ffn_g[l]), peer_wq[l], peer_k1[l], peer_k2[l], peer_u[l], peer_v[l])
    return rmsnorm(h, ln_f_g)
```

```python
import functools

import jax
import jax.numpy as jnp
from jax import lax
from jax.experimental import pallas as pl
from jax.experimental.pallas import tpu as pltpu

F32 = jnp.float32
BF16 = jnp.bfloat16

EPS = 1e-6
M_HEADS = 4
M_CONV = 4
M_CHUNK = 64
N_Q_HEADS = 8
N_KV_GROUPS = 2
N_HPG = N_Q_HEADS // N_KV_GROUPS
N_HEAD_DIM = 128
CMP_BLOCK = 32
CMP_STRIDE = 16
SLC_BLOCK = 64
SLC_TOP = 8
WINDOW = 512
Q_BLOCK = 128
ROPE_THETA = 10000.0
P_HEADS = 8
P_KEYS = 128
P_HALF = 64
P_TOPK = 16

LANES = 128
VMEM_LIMIT = 56 * 1024 * 1024
NEG = -1e30

_NT = (((1,), (1,)), ((), ()))
_TN = (((0,), (0,)), ((), ()))


def _cparams(n_axes):
    return pltpu.CompilerParams(dimension_semantics=("arbitrary",) * n_axes,
                                vmem_limit_bytes=VMEM_LIMIT)


def _rms(x, g):
    return x * lax.rsqrt(jnp.mean(x * x, axis=-1, keepdims=True) + EPS) * g


def _gelu(x):
    return 0.5 * x * (1.0 + lax.erf(x * (2.0 ** -0.5)))


def _in_proj_kernel(x_ref, g_ref, w_ref, *o_refs, widths):
    xn = _rms(x_ref[...], g_ref[...]).astype(BF16)
    off = 0
    for o_ref, w in zip(o_refs, widths):
        o_ref[...] = jnp.dot(xn, w_ref[:, off:off + w],
                             preferred_element_type=F32).astype(o_ref.dtype)
        off += w


def _in_proj(x2d, g, w, widths, dtypes, tm):
    T, D = x2d.shape
    N = w.shape[1]
    return pl.pallas_call(
        functools.partial(_in_proj_kernel, widths=tuple(widths)),
        grid=(T // tm,),
        in_specs=[pl.BlockSpec((tm, D), lambda i: (i, 0)),
                  pl.BlockSpec((1, D), lambda i: (0, 0)),
                  pl.BlockSpec((D, N), lambda i: (0, 0), pipeline_mode=pl.Buffered(1))],
        out_specs=[pl.BlockSpec((tm, wd), lambda i: (i, 0)) for wd in widths],
        out_shape=[jax.ShapeDtypeStruct((T, wd), dt) for wd, dt in zip(widths, dtypes)],
        compiler_params=_cparams(1),
        name="in_proj",
    )(x2d, g.reshape(1, D), w)


def _log_sigmoid(x):
    return jnp.minimum(x, 0.0) - jnp.log1p(jnp.exp(-jnp.abs(x)))


def _mlstm_kernel(qk_ref, v_ref, og_ref, gcol_ref, grow_ref, brow_ref, bcol_ref, cw_ref, cb_ref,
                  ng_ref, out_ref, ext_ref, ct_ref, n_ref, m_ref, *, L, H, dh):
    W = H * dh

    @pl.when(pl.program_id(1) == 0)
    def _init():
        ext_ref[0:8, :] = jnp.zeros((8, 2 * W), F32)
        ct_ref[...] = jnp.zeros_like(ct_ref)
        n_ref[...] = jnp.zeros_like(n_ref)
        m_ref[...] = jnp.zeros_like(m_ref)

    x = qk_ref[...]
    ext_ref[8:8 + L, :] = x
    pre = cb_ref[...] + cw_ref[0:1, :] * ext_ref[8 - (M_CONV - 1):8 - (M_CONV - 1) + L, :]
    for j in range(1, M_CONV):
        o = 8 - (M_CONV - 1) + j
        pre = pre + cw_ref[j:j + 1, :] * ext_ref[o:o + L, :]
    ext_ref[0:8, :] = x[L - 8:L, :]
    qk = pre * jax.nn.sigmoid(pre)

    gcol = gcol_ref[...] + brow_ref[...]
    grow = grow_ref[...] + bcol_ref[...]
    rr = lax.broadcasted_iota(jnp.int32, (L, L), 0)
    cc = lax.broadcasted_iota(jnp.int32, (L, L), 1)
    causal = cc <= rr
    lower = causal.astype(F32)

    for h in range(H):
        q = qk[:, h * dh:(h + 1) * dh]
        k = qk[:, W + h * dh:W + (h + 1) * dh] * (dh ** -0.5)
        v = v_ref[:, h * dh:(h + 1) * dh].astype(F32)
        li_c = gcol[:, h:h + 1]
        lf_c = _log_sigmoid(gcol[:, H + h:H + h + 1])
        li_r = grow[h:h + 1, :]
        lf_r = _log_sigmoid(grow[H + h:H + h + 1, :])
        b_c = jnp.sum(lower * lf_r, axis=1, keepdims=True)
        b_r = jnp.sum((rr <= cc).astype(F32) * lf_c, axis=0, keepdims=True)
        m_prev = m_ref[h:h + 1, 0:1]

        dm = jnp.where(causal, b_c - b_r + li_r, -jnp.inf)
        inter = b_c + m_prev
        mt = jnp.maximum(inter, jnp.max(dm, axis=1, keepdims=True))
        dw = jnp.exp(dm - mt)
        iw = jnp.exp(inter - mt)
        qb = q.astype(BF16)
        kb = k.astype(BF16)
        wm = lax.dot_general(qb, kb, _NT, preferred_element_type=F32) * dw
        ct = ct_ref[h]
        n_row = n_ref[h:h + 1, :]
        num = (iw * jnp.dot(qb, ct.astype(BF16), preferred_element_type=F32)
               + jnp.dot(wm.astype(BF16), v.astype(BF16), preferred_element_type=F32))
        den = iw * jnp.sum(q * n_row, axis=1, keepdims=True) + jnp.sum(wm, axis=1, keepdims=True)
        hh = num / jnp.maximum(jnp.abs(den), jnp.exp(-mt))

        b_last = b_c[L - 1:L, :]
        g_r = b_last - b_r + li_r
        g_c = b_last - b_c + li_c
        m_new = jnp.maximum(b_last + m_prev, jnp.max(g_r, axis=1, keepdims=True))
        a = jnp.exp(b_last + m_prev - m_new)
        w_c = jnp.exp(g_c - m_new)
        ct_ref[h] = a * ct + lax.dot_general(kb, (v * w_c).astype(BF16), _TN,
                                             preferred_element_type=F32)
        n_ref[h:h + 1, :] = a * n_row + jnp.sum(k * w_c, axis=0, keepdims=True)
        m_ref[h:h + 1, :] = jnp.broadcast_to(m_new, (1, LANES))

        hn = hh * lax.rsqrt(jnp.mean(hh * hh, axis=1, keepdims=True) + EPS)
        o = hn * ng_ref[:, h * dh:(h + 1) * dh] * jax.nn.sigmoid(og_ref[:, h * dh:(h + 1) * dh])
        out_ref[:, h * dh:(h + 1) * dh] = o.astype(out_ref.dtype)


def _mlstm(m_qk, m_v, m_o, small, grow, brow, bcol, conv_w, conv_b, norm_g):
    B, S, W2 = m_qk.shape
    W = W2 // 2
    H, L = M_HEADS, M_CHUNK
    dh = W // H
    nc = S // L
    return pl.pallas_call(
        functools.partial(_mlstm_kernel, L=L, H=H, dh=dh),
        grid=(B, nc),
        in_specs=[pl.BlockSpec((None, L, W2), lambda b, c: (b, c, 0)),
                  pl.BlockSpec((None, L, W), lambda b, c: (b, c, 0)),
                  pl.BlockSpec((None, L, W), lambda b, c: (b, c, 0)),
                  pl.BlockSpec((None, L, LANES), lambda b, c: (b, c, 0)),
                  pl.BlockSpec((None, None, 8, L), lambda b, c: (b, c, 0, 0)),
                  pl.BlockSpec((1, LANES), lambda b, c: (0, 0)),
                  pl.BlockSpec((8, 1), lambda b, c: (0, 0)),
                  pl.BlockSpec((M_CONV, W2), lambda b, c: (0, 0)),
                  pl.BlockSpec((1, W2), lambda b, c: (0, 0)),
                  pl.BlockSpec((1, W), lambda b, c: (0, 0))],
        out_specs=pl.BlockSpec((None, L, W), lambda b, c: (b, c, 0)),
        out_shape=jax.ShapeDtypeStruct((B, S, W), BF16),
        scratch_shapes=[pltpu.VMEM((8 + L, W2), F32),
                        pltpu.VMEM((H, dh, dh), F32),
                        pltpu.VMEM((8, dh), F32),
                        pltpu.VMEM((8, LANES), F32)],
        compiler_params=_cparams(2),
        name="mlstm",
    )(m_qk, m_v, m_o, small, grow, brow, bcol, conv_w, conv_b, norm_g)


def _rope(x, cos, sin_signed):
    return x * cos + pltpu.roll(x, N_HEAD_DIM // 2, axis=1) * sin_signed


def _nsa_prep_kernel(q_ref, ks_ref, kw_ref, cos_ref, sin_ref, qs_out, qr_out, ks_out, kw_out):
    cos = cos_ref[...]
    sin = sin_ref[...]
    scale = N_HEAD_DIM ** -0.5
    d = N_HEAD_DIM
    for h in range(N_Q_HEADS):
        x = q_ref[:, h * d:(h + 1) * d] * scale
        qs_out[:, h * d:(h + 1) * d] = x.astype(BF16)
        qr_out[:, h * d:(h + 1) * d] = _rope(x, cos, sin).astype(BF16)
    for g in range(N_KV_GROUPS):
        ks_out[:, g * d:(g + 1) * d] = _rope(ks_ref[:, g * d:(g + 1) * d], cos, sin).astype(BF16)
        kw_out[:, g * d:(g + 1) * d] = _rope(kw_ref[:, g * d:(g + 1) * d], cos, sin).astype(BF16)


def _nsa_prep(n_q, n_ks, n_kw, cos, sin_signed, ts):
    B, S, NW = n_q.shape
    KW = n_ks.shape[-1]
    tok = lambda w: pl.BlockSpec((None, ts, w), lambda b, i: (b, i, 0))
    tab = pl.BlockSpec((ts, N_HEAD_DIM), lambda b, i: (i, 0))
    return pl.pallas_call(
        _nsa_prep_kernel,
        grid=(B, S // ts),
        in_specs=[tok(NW), tok(KW), tok(KW), tab, tab],
        out_specs=[tok(NW), tok(NW), tok(KW), tok(KW)],
        out_shape=[jax.ShapeDtypeStruct((B, S, NW), BF16), jax.ShapeDtypeStruct((B, S, NW), BF16),
                   jax.ShapeDtypeStruct((B, S, KW), BF16), jax.ShapeDtypeStruct((B, S, KW), BF16)],
        compiler_params=_cparams(2),
        name="nsa_prep",
    )(n_q, n_ks, n_kw, cos, sin_signed)


def _compress_kernel(x_ref, p_ref, wh_ref, w2_ref, o_ref):
    x = x_ref[...]
    ns = x.shape[0]
    y0 = jnp.dot((x + p_ref[0:1, :]).astype(BF16), wh_ref[0], preferred_element_type=F32)
    y1 = jnp.dot((x + p_ref[1:2, :]).astype(BF16), wh_ref[1], preferred_element_type=F32)
    act = _gelu(y0 + pltpu.roll(y1, ns - 1, axis=0))
    d = N_HEAD_DIM
    for g in range(N_KV_GROUPS):
        o_ref[:, g * d:(g + 1) * d] = jnp.dot(act[:, g * d:(g + 1) * d].astype(BF16), w2_ref[...],
                                              preferred_element_type=F32).astype(o_ref.dtype)


def _compress(x_seg, pos2, wh, w2):
    B, NS, SW = x_seg.shape
    GW = N_KV_GROUPS * N_HEAD_DIM
    return pl.pallas_call(
        _compress_kernel,
        grid=(B,),
        in_specs=[pl.BlockSpec((None, NS, SW), lambda b: (b, 0, 0)),
                  pl.BlockSpec((2, SW), lambda b: (0, 0)),
                  pl.BlockSpec((2, SW, GW), lambda b: (0, 0, 0)),
                  pl.BlockSpec((N_HEAD_DIM, N_HEAD_DIM), lambda b: (0, 0))],
        out_specs=pl.BlockSpec((None, NS, GW), lambda b: (b, 0, 0)),
        out_shape=jax.ShapeDtypeStruct((B, NS, GW), BF16),
        compiler_params=_cparams(1),
        name="nsa_compress",
    )(x_seg, pos2, wh, w2)


def _nsa_attn_kernel(qs_ref, qr_ref, kc_ref, vc_ref, ks_ref, vs_ref, kw_ref, vw_ref, gate_ref,
                     ov_ref, ex_ref, o_ref, m_ref, l_ref, acc_ref, *, TQ, TK, S, n_slc):
    i = pl.program_id(2)
    t0 = i * TQ
    d = N_HEAD_DIM
    NC = kc_ref.shape[0]

    trow = t0 + lax.broadcasted_iota(jnp.int32, (TQ, NC), 0)
    ncol = lax.broadcasted_iota(jnp.int32, (TQ, NC), 1)
    cmask = (ncol * CMP_STRIDE + (CMP_BLOCK - 1)) <= trow
    kc = kc_ref[...]
    vc = vc_ref[...]
    psum = jnp.zeros((TQ, NC), F32)
    o_cmp = []
    for hh in range(N_HPG):
        s = lax.dot_general(qs_ref[:, hh * d:(hh + 1) * d], kc, _NT, preferred_element_type=F32)
        s = jnp.where(cmask, s, NEG)
        p = jnp.where(cmask, jnp.exp(s - jnp.max(s, axis=1, keepdims=True)), 0.0)
        l = jnp.sum(p, axis=1, keepdims=True)
        p = p / jnp.where(l > 0.0, l, 1.0)
        psum = psum + p
        o_cmp.append(jnp.dot(p.astype(BF16), vc, preferred_element_type=F32))

    imp = jnp.dot(psum, ov_ref[...], preferred_element_type=F32, precision=lax.Precision.HIGHEST)
    jl = lax.broadcasted_iota(jnp.int32, (TQ, LANES), 1)
    tr = t0 + lax.broadcasted_iota(jnp.int32, (TQ, LANES), 0)
    cur = jnp.right_shift(tr, SLC_BLOCK.bit_length() - 1)
    forced = (jl == 0) | (jl == cur) | (jl == cur - 1)
    valid = (jl * SLC_BLOCK) <= tr
    score = jnp.where(forced, 1e9, jnp.where(valid, imp, -1e9))
    score = jnp.where(jl < n_slc, score, -3e9)
    sel = jnp.zeros((TQ, LANES), F32)
    for _ in range(SLC_TOP):
        mx = jnp.max(score, axis=1, keepdims=True)
        idx = jnp.min(jnp.where(score == mx, jl, 1 << 30), axis=1, keepdims=True)
        hit = jl == idx
        sel = jnp.where(hit & (mx > -1e8), 1.0, sel)
        score = jnp.where(hit, -3e38, score)
    selb = sel.astype(BF16)

    m_ref[...] = jnp.full(m_ref.shape, NEG, F32)
    l_ref[...] = jnp.zeros(l_ref.shape, F32)
    acc_ref[...] = jnp.zeros(acc_ref.shape, F32)
    tq_k = t0 + lax.broadcasted_iota(jnp.int32, (TQ, TK), 0)
    kcol = lax.broadcasted_iota(jnp.int32, (TQ, TK), 1)

    def sweep(kt, carry):
        k0 = pl.multiple_of(kt * TK, TK)
        k = ks_ref[pl.ds(k0, TK), :]
        v = vs_ref[pl.ds(k0, TK), :]
        bm = jnp.dot(selb, ex_ref[kt], preferred_element_type=F32)
        mask = (bm > 0.5) & ((k0 + kcol) <= tq_k)
        for hh in range(N_HPG):
            rows = slice(hh * TQ, (hh + 1) * TQ)
            s = lax.dot_general(qr_ref[:, hh * d:(hh + 1) * d], k, _NT, preferred_element_type=F32)
            s = jnp.where(mask, s, NEG)
            m_prev = m_ref[rows, :]
            m_new = jnp.maximum(m_prev, jnp.max(s, axis=1, keepdims=True))
            alpha = jnp.exp(m_prev - m_new)
            p = jnp.where(mask, jnp.exp(s - m_new[:, 0:1]), 0.0)
            l_ref[rows, :] = alpha * l_ref[rows, :] + jnp.sum(p, axis=1, keepdims=True)
            acc_ref[rows, :] = alpha * acc_ref[rows, :] + jnp.dot(p.astype(BF16), v,
                                                                   preferred_element_type=F32)
            m_ref[rows, :] = m_new
        return carry

    lax.fori_loop(0, (t0 + TQ + TK - 1) // TK, sweep, 0)

    span = WINDOW + TQ
    ws = pl.multiple_of(jnp.maximum(t0 - WINDOW, 0), TQ)
    kwin = kw_ref[pl.ds(ws, span), :]
    vwin = vw_ref[pl.ds(ws, span), :]
    kp = ws + lax.broadcasted_iota(jnp.int32, (TQ, span), 1)
    tq_w = t0 + lax.broadcasted_iota(jnp.int32, (TQ, span), 0)
    wmask = (kp <= tq_w) & (kp > tq_w - WINDOW)

    gate = jax.nn.sigmoid(gate_ref[...])
    for hh in range(N_HPG):
        rows = slice(hh * TQ, (hh + 1) * TQ)
        l = l_ref[rows, :]
        o_slc = acc_ref[rows, :] / jnp.where(l > 0.0, l, 1.0)
        s = lax.dot_general(qr_ref[:, hh * d:(hh + 1) * d], kwin, _NT, preferred_element_type=F32)
        s = jnp.where(wmask, s, NEG)
        p = jnp.where(wmask, jnp.exp(s - jnp.max(s, axis=1, keepdims=True)), 0.0)
        lw = jnp.sum(p, axis=1, keepdims=True)
        o_win = jnp.dot(p.astype(BF16), vwin, preferred_element_type=F32) / jnp.where(lw > 0.0, lw, 1.0)
        o = (gate[:, 3 * hh:3 * hh + 1] * o_cmp[hh] + gate[:, 3 * hh + 1:3 * hh + 2] * o_slc
             + gate[:, 3 * hh + 2:3 * hh + 3] * o_win)
        o_ref[:, hh * d:(hh + 1) * d] = o.astype(o_ref.dtype)


def _nsa_attn(q_s, q_r, kcmp, vcmp, ks_r, vs, kw_r, vw, gates, overlap, expand, tk):
    B, S, _ = q_s.shape
    G, d, TQ = N_KV_GROUPS, N_HEAD_DIM, Q_BLOCK
    NC = kcmp.shape[1]
    gw = N_HPG * d
    qspec = pl.BlockSpec((None, TQ, gw), lambda b, g, i: (b, i, g))
    cspec = pl.BlockSpec((None, NC, d), lambda b, g, i: (b, 0, g))
    kspec = pl.BlockSpec((None, S, d), lambda b, g, i: (b, 0, g))
    return pl.pallas_call(
        functools.partial(_nsa_attn_kernel, TQ=TQ, TK=tk, S=S, n_slc=S // SLC_BLOCK),
        grid=(B, G, S // TQ),
        in_specs=[qspec, qspec, cspec, cspec, kspec, kspec, kspec, kspec,
                  pl.BlockSpec((None, TQ, LANES), lambda b, g, i: (b, i, g)),
                  pl.BlockSpec(overlap.shape, lambda b, g, i: (0, 0)),
                  pl.BlockSpec(expand.shape, lambda b, g, i: (0, 0, 0))],
        out_specs=qspec,
        out_shape=jax.ShapeDtypeStruct((B, S, G * gw), BF16),
        scratch_shapes=[pltpu.VMEM((N_HPG * TQ, LANES), F32),
                        pltpu.VMEM((N_HPG * TQ, LANES), F32),
                        pltpu.VMEM((N_HPG * TQ, d), F32)],
        compiler_params=_cparams(3),
        name="nsa_attn",
    )(q_s, q_r, kcmp, vcmp, ks_r, vs, kw_r, vw, gates, overlap, expand)


def _mix_kernel(hm_ref, on_ref, ga_ref, gb_ref, x_ref, wm_ref, wn_ref, wo_ref, g2_ref,
                h1_ref, xn2_ref):
    ya = jnp.dot(hm_ref[...], wm_ref[...], preferred_element_type=F32)
    yb = jnp.dot(on_ref[...], wn_ref[...], preferred_element_type=F32)
    mix = jax.nn.sigmoid(ga_ref[...]) * ya + jax.nn.sigmoid(gb_ref[...]) * yb
    h1 = x_ref[...] + jnp.dot(mix.astype(BF16), wo_ref[...], preferred_element_type=F32)
    h1_ref[...] = h1
    xn2_ref[...] = _rms(h1, g2_ref[...]).astype(BF16)


def _mix(hm, on, g_a, g_b, x2d, w_m, w_n, w_o, g2, tm):
    T, D = x2d.shape
    tok = pl.BlockSpec((tm, D), lambda i: (i, 0))
    wsp = pl.BlockSpec((D, D), lambda i: (0, 0))
    return pl.pallas_call(
        _mix_kernel,
        grid=(T // tm,),
        in_specs=[tok, tok, tok, tok, tok, wsp, wsp, wsp, pl.BlockSpec((1, D), lambda i: (0, 0))],
        out_specs=[tok, tok],
        out_shape=[jax.ShapeDtypeStruct((T, D), F32), jax.ShapeDtypeStruct((T, D), BF16)],
        compiler_params=_cparams(1),
        name="mix",
    )(hm, on, g_a, g_b, x2d, w_m, w_n, w_o, g2.reshape(1, D))


def _top_values(s, k):
    vals = []
    for _ in range(k):
        m = jnp.max(s, axis=0, keepdims=True)
        vals.append(m)
        s = jnp.where(s == m, -jnp.inf, s)
    return vals


def _peer_score_kernel(xn_ref, wq_ref, k1_ref, k2_ref, a_ref, c_ref, s2_ref, e2_ref):
    q = jnp.dot(xn_ref[...], wq_ref[...], preferred_element_type=F32)
    nk = P_KEYS
    for h in range(P_HEADS):
        qh = q[:, h * 2 * P_HALF:(h + 1) * 2 * P_HALF].astype(BF16)
        s1 = lax.dot_general(k1_ref[...], qh, _NT, preferred_element_type=F32)
        s2 = lax.dot_general(k2_ref[...], qh, _NT, preferred_element_type=F32)
        v1 = _top_values(s1, P_TOPK)
        v2 = _top_values(s2, P_TOPK)
        v2m = jnp.concatenate(v2, axis=0)
        cand = jnp.concatenate([v1[a] + v2m for a in range(P_TOPK)], axis=0)
        top = _top_values(cand, P_TOPK)
        z = jnp.exp(top[0] - top[0])
        for t in top[1:]:
            z = z + jnp.exp(t - top[0])
        cut = jnp.full(s1.shape, jnp.inf, F32)
        for b in range(P_TOPK):
            cut = jnp.where(s1 + v2[b] >= top[-1], jnp.minimum(cut, v2[b]), cut)
        rows = slice(h * nk, (h + 1) * nk)
        a_ref[rows, :] = cut
        c_ref[rows, :] = jnp.exp(s1 - v1[0]) / z
        s2_ref[rows, :] = s2
        e2_ref[rows, :] = jnp.exp(s2 - v2[0])


def _peer_score(xn2, wq, k1p, k2p, tt):
    T, D = xn2.shape
    nt = T // tt
    R = P_HEADS * P_KEYS
    osp = pl.BlockSpec((None, R, tt), lambda i: (i, 0, 0))
    osh = jax.ShapeDtypeStruct((nt, R, tt), F32)
    return pl.pallas_call(
        _peer_score_kernel,
        grid=(nt,),
        in_specs=[pl.BlockSpec((tt, D), lambda i: (i, 0)),
                  pl.BlockSpec(wq.shape, lambda i: (0, 0)),
                  pl.BlockSpec(k1p.shape, lambda i: (0, 0)),
                  pl.BlockSpec(k2p.shape, lambda i: (0, 0))],
        out_specs=[osp, osp, osp, osp],
        out_shape=[osh, osh, osh, osh],
        compiler_params=_cparams(1),
        name="peer_score",
    )(xn2, wq, k1p, k2p)


PEER_ROWS = 32
PEER_COLS = 256


def _peer_dense_kernel(xn_ref, a_ref, c_ref, s2_ref, e2_ref, u_ref, vt_ref, h1_ref, gf_ref,
                       out_ref, act_ref, p_ref, acc_ref, *, EC, TT):
    ci = pl.program_id(1)
    nk = P_KEYS
    n_i1 = EC // nk

    @pl.when(ci == 0)
    def _init():
        acc_ref[...] = jnp.zeros_like(acc_ref)

    act_ref[...] = lax.dot_general(u_ref[...], xn_ref[...], _NT, preferred_element_type=F32)

    def per_i1(il, carry):
        i1 = ci * n_i1 + il
        r0 = pl.multiple_of(il * nk, nk)
        for lb in range(TT // PEER_COLS):
            lanes = slice(lb * PEER_COLS, (lb + 1) * PEER_COLS)
            for rb in range(nk // PEER_ROWS):
                w = jnp.zeros((PEER_ROWS, PEER_COLS), F32)
                for h in range(P_HEADS):
                    rows = slice(h * nk + rb * PEER_ROWS, h * nk + (rb + 1) * PEER_ROWS)
                    thr = a_ref[pl.ds(h * nk + i1, 1), lanes]
                    coef = c_ref[pl.ds(h * nk + i1, 1), lanes]
                    w = w + jnp.where(s2_ref[rows, lanes] >= thr, coef * e2_ref[rows, lanes], 0.0)
                dst = pl.ds(r0 + rb * PEER_ROWS, PEER_ROWS)
                p_ref[dst, lanes] = (w * _gelu(act_ref[dst, lanes])).astype(BF16)
        return carry

    lax.fori_loop(0, n_i1, per_i1, 0)
    acc_ref[...] += jnp.dot(vt_ref[...], p_ref[...], preferred_element_type=F32)

    @pl.when(ci == pl.num_programs(1) - 1)
    def _finish():
        y = h1_ref[...] + acc_ref[...].T
        out_ref[...] = _rms(y, gf_ref[...])


def _peer_dense(xn2, a, c, s2, e2, u, vt, h1, gf, tt, ec):
    T, D = xn2.shape
    E = u.shape[0]
    R = P_HEADS * P_KEYS
    ssp = pl.BlockSpec((None, R, tt), lambda t, k: (t, 0, 0))
    tok = pl.BlockSpec((tt, D), lambda t, k: (t, 0))
    return pl.pallas_call(
        functools.partial(_peer_dense_kernel, EC=ec, TT=tt),
        grid=(T // tt, E // ec),
        in_specs=[tok, ssp, ssp, ssp, ssp,
                  pl.BlockSpec((ec, D), lambda t, k: (k, 0)),
                  pl.BlockSpec((D, ec), lambda t, k: (0, k)),
                  tok,
                  pl.BlockSpec((1, D), lambda t, k: (0, 0))],
        out_specs=tok,
        out_shape=jax.ShapeDtypeStruct((T, D), F32),
        scratch_shapes=[pltpu.VMEM((ec, tt), F32),
                        pltpu.VMEM((ec, tt), BF16),
                        pltpu.VMEM((D, tt), F32)],
        compiler_params=_cparams(2),
        name="peer_dense",
    )(xn2, a, c, s2, e2, u, vt, h1, gf.reshape(1, D))


def _pad_cols(w, width):
    return jnp.pad(w, ((0, 0), (0, width - w.shape[1])))


def _segment_weights(w1):
    d, G = N_HEAD_DIM, N_KV_GROUPS
    half = CMP_BLOCK // 2
    w = w1.reshape(2, half, d, d)
    eye = jnp.eye(G, dtype=w1.dtype)
    w = w[:, :, None, :, None, :] * eye[None, None, :, None, :, None]
    return w.reshape(2, half * G * d, G * d)


def _segment_pos(pos):
    d, G = N_HEAD_DIM, N_KV_GROUPS
    half = CMP_BLOCK // 2
    p = jnp.broadcast_to(pos.reshape(2, half, 1, d), (2, half, G, d))
    return p.reshape(2, half * G * d)


def _forward(x, ln_mix_g, w_in, m_conv_w, m_conv_b, m_i_bias, m_f_bias, m_norm_g, w_m_out, cmp_k_pos, cmp_k_w1, cmp_k_w2, cmp_v_pos, cmp_v_w1, cmp_v_w2, w_n_out, w_out, ln_ffn_g, peer_wq, peer_k1, peer_k2, peer_u, peer_v, ln_f_g):
    B, S, D = x.shape
    T = B * S
    assert ln_mix_g.shape[0] == 1, "the final norm is fused into the last stage of a single layer"
    assert S % Q_BLOCK == 0 and S >= WINDOW + Q_BLOCK and S // SLC_BLOCK >= SLC_TOP
    l = 0
    d, G, H = N_HEAD_DIM, N_KV_GROUPS, M_HEADS
    MW = D
    KVW = G * d
    h = x.reshape(T, D)

    inv = ROPE_THETA ** (-jnp.arange(0, d, 2, dtype=F32) / d)
    ang = jnp.arange(S, dtype=F32)[:, None] * inv[None, :]
    ang = jnp.concatenate([ang, ang], axis=-1)
    cos = jnp.cos(ang)
    sin_signed = jnp.sin(ang) * jnp.concatenate([-jnp.ones((d // 2,), F32), jnp.ones((d // 2,), F32)])
    ncp = S // CMP_STRIDE
    n_idx = jnp.arange(ncp)[:, None] * CMP_STRIDE
    j_idx = jnp.arange(LANES)[None, :] * SLC_BLOCK
    overlap = ((n_idx < j_idx + SLC_BLOCK) & (n_idx + CMP_BLOCK > j_idx)
               & (jnp.arange(LANES)[None, :] < S // SLC_BLOCK)).astype(F32)
    tk = min(512, S)
    expand = (jnp.arange(LANES)[None, :, None]
              == (jnp.arange(S).reshape(S // tk, 1, tk) // SLC_BLOCK)).astype(BF16)

    wl = w_in[l]
    offs = [0]
    for wdt in (2 * MW, MW, MW, H, H, D, KVW, KVW, KVW, KVW, KVW, KVW, 3 * N_Q_HEADS, D, D):
        offs.append(offs[-1] + wdt)
    col = lambda i: wl[:, offs[i]:offs[i + 1]]
    w_small = _pad_cols(jnp.concatenate([col(3), col(4)], axis=1), LANES)
    w_ng = col(12).reshape(D, G, 3 * N_HPG)
    w_ng = jnp.pad(w_ng, ((0, 0), (0, 0), (0, LANES - 3 * N_HPG))).reshape(D, G * LANES)
    groups = [(col(0), F32), (col(1), BF16), (col(2), F32), (col(5), F32),
              (col(6), F32), (col(7), F32), (col(8), F32), (col(9), BF16),
              (col(10), F32), (col(11), BF16), (col(13), F32), (col(14), F32),
              (w_small, F32), (w_ng, F32)]
    w_cat = jnp.concatenate([g[0] for g in groups], axis=1).astype(BF16)
    (m_qk, m_v, m_o, n_q, n_kc, n_vc, n_ks, n_vs, n_kw, n_vw, g_a, g_b, small, n_g) = _in_proj(
        h, ln_mix_g[l], w_cat, [g[0].shape[1] for g in groups], [g[1] for g in groups], tm=256)

    small3 = small.reshape(B, S, LANES)
    grow = small3[:, :, :8].reshape(B, S // M_CHUNK, M_CHUNK, 8).transpose(0, 1, 3, 2)
    bias = jnp.concatenate([m_i_bias[l], m_f_bias[l]])
    hm = _mlstm(m_qk.reshape(B, S, 2 * MW), m_v.reshape(B, S, MW), m_o.reshape(B, S, MW),
                small3, grow, _pad_cols(bias.reshape(1, 2 * H), LANES), bias.reshape(2 * H, 1),
                m_conv_w[l], m_conv_b[l].reshape(1, 2 * MW), m_norm_g[l].reshape(1, MW))

    q_s, q_r, ks_r, kw_r = _nsa_prep(n_q.reshape(B, S, D), n_ks.reshape(B, S, KVW),
                                     n_kw.reshape(B, S, KVW), cos, sin_signed, ts=min(512, S))
    seg = CMP_STRIDE * KVW
    kcmp = _compress(n_kc.reshape(B, ncp, seg), _segment_pos(cmp_k_pos[l]),
                     _segment_weights(cmp_k_w1[l]).astype(BF16), cmp_k_w2[l].astype(BF16))
    vcmp = _compress(n_vc.reshape(B, ncp, seg), _segment_pos(cmp_v_pos[l]),
                     _segment_weights(cmp_v_w1[l]).astype(BF16), cmp_v_w2[l].astype(BF16))
    on = _nsa_attn(q_s, q_r, kcmp, vcmp, ks_r, n_vs.reshape(B, S, KVW), kw_r,
                   n_vw.reshape(B, S, KVW), n_g.reshape(B, S, G * LANES), overlap, expand, tk)

    h1, xn2 = _mix(hm.reshape(T, MW), on.reshape(T, D), g_a, g_b, h,
                   w_m_out[l].astype(BF16), w_n_out[l].astype(BF16), w_out[l].astype(BF16),
                   ln_ffn_g[l], tm=256)

    tt = min(512, T)
    k1p = jnp.pad(peer_k1[l], ((0, 0), (0, P_HALF))).astype(BF16)
    k2p = jnp.pad(peer_k2[l], ((0, 0), (P_HALF, 0))).astype(BF16)
    a, c, s2, e2 = _peer_score(xn2, peer_wq[l].astype(BF16), k1p, k2p, tt)
    out = _peer_dense(xn2, a, c, s2, e2, peer_u[l].astype(BF16), peer_v[l].T.astype(BF16),
                      h1, ln_f_g, tt, ec=1024)
    aux = dict(m_qk=m_qk, m_v=m_v, small=small, hm=hm, q_s=q_s, q_r=q_r, ks_r=ks_r, kw_r=kw_r,
               kcmp=kcmp, vcmp=vcmp, on=on, h1=h1, xn2=xn2, a=a, c=c, s2=s2, e2=e2)
    return out.reshape(B, S, D), aux


def kernel(x, ln_mix_g, w_in, m_conv_w, m_conv_b, m_i_bias, m_f_bias, m_norm_g, w_m_out, cmp_k_pos, cmp_k_w1, cmp_k_w2, cmp_v_pos, cmp_v_w1, cmp_v_w2, w_n_out, w_out, ln_ffn_g, peer_wq, peer_k1, peer_k2, peer_u, peer_v, ln_f_g):
    return _forward(x, ln_mix_g, w_in, m_conv_w, m_conv_b, m_i_bias, m_f_bias, m_norm_g, w_m_out,
                    cmp_k_pos, cmp_k_w1, cmp_k_w2, cmp_v_pos, cmp_v_w1, cmp_v_w2, w_n_out, w_out,
                    ln_ffn_g, peer_wq, peer_k1, peer_k2, peer_u, peer_v, ln_f_g)[0]
```

```python
import functools

import jax
import jax.numpy as jnp
from jax import lax
from jax.experimental import pallas as pl
from jax.experimental.pallas import tpu as pltpu

F32 = jnp.float32
BF16 = jnp.bfloat16

EPS = 1e-6
M_HEADS = 4
M_CONV = 4
M_CHUNK = 64
N_Q_HEADS = 8
N_KV_GROUPS = 2
N_HPG = N_Q_HEADS // N_KV_GROUPS
N_HEAD_DIM = 128
CMP_BLOCK = 32
CMP_STRIDE = 16
SLC_BLOCK = 64
SLC_TOP = 8
WINDOW = 512
Q_BLOCK = 128
ROPE_THETA = 10000.0
P_HEADS = 8
P_KEYS = 128
P_HALF = 64
P_TOPK = 16

LANES = 128
VMEM_LIMIT = 56 * 1024 * 1024
NEG = -1e30

_NT = (((1,), (1,)), ((), ()))
_TN = (((0,), (0,)), ((), ()))


def _cparams(n_axes):
    return pltpu.CompilerParams(dimension_semantics=("arbitrary",) * n_axes,
                                vmem_limit_bytes=VMEM_LIMIT)


def _rms(x, g):
    return x * lax.rsqrt(jnp.mean(x * x, axis=-1, keepdims=True) + EPS) * g


def _gelu(x):
    return 0.5 * x * (1.0 + lax.erf(x * (2.0 ** -0.5)))


def _in_proj_kernel(x_ref, g_ref, w_ref, *o_refs, widths):
    xn = _rms(x_ref[...], g_ref[...]).astype(BF16)
    off = 0
    for o_ref, w in zip(o_refs, widths):
        o_ref[...] = jnp.dot(xn, w_ref[:, off:off + w],
                             preferred_element_type=F32).astype(o_ref.dtype)
        off += w


def _in_proj(x2d, g, w, widths, dtypes, tm):
    T, D = x2d.shape
    N = w.shape[1]
    return pl.pallas_call(
        functools.partial(_in_proj_kernel, widths=tuple(widths)),
        grid=(T // tm,),
        in_specs=[pl.BlockSpec((tm, D), lambda i: (i, 0)),
                  pl.BlockSpec((1, D), lambda i: (0, 0)),
                  pl.BlockSpec((D, N), lambda i: (0, 0), pipeline_mode=pl.Buffered(1))],
        out_specs=[pl.BlockSpec((tm, wd), lambda i: (i, 0)) for wd in widths],
        out_shape=[jax.ShapeDtypeStruct((T, wd), dt) for wd, dt in zip(widths, dtypes)],
        compiler_params=_cparams(1),
        name="in_proj",
    )(x2d, g.reshape(1, D), w)


def _log_sigmoid(x):
    return jnp.minimum(x, 0.0) - jnp.log1p(jnp.exp(-jnp.abs(x)))


def _mlstm_kernel(qk_ref, v_ref, og_ref, gcol_ref, grow_ref, brow_ref, bcol_ref, cw_ref, cb_ref,
                  ng_ref, out_ref, ext_ref, ct_ref, n_ref, m_ref, *, L, H, dh):
    W = H * dh

    @pl.when(pl.program_id(1) == 0)
    def _init():
        ext_ref[0:8, :] = jnp.zeros((8, 2 * W), F32)
        ct_ref[...] = jnp.zeros_like(ct_ref)
        n_ref[...] = jnp.zeros_like(n_ref)
        m_ref[...] = jnp.zeros_like(m_ref)

    x = qk_ref[...]
    ext_ref[8:8 + L, :] = x
    pre = cb_ref[...] + cw_ref[0:1, :] * ext_ref[8 - (M_CONV - 1):8 - (M_CONV - 1) + L, :]
    for j in range(1, M_CONV):
        o = 8 - (M_CONV - 1) + j
        pre = pre + cw_ref[j:j + 1, :] * ext_ref[o:o + L, :]
    ext_ref[0:8, :] = x[L - 8:L, :]
    qk = pre * jax.nn.sigmoid(pre)

    gcol = gcol_ref[...] + brow_ref[...]
    grow = grow_ref[...] + bcol_ref[...]
    rr = lax.broadcasted_iota(jnp.int32, (L, L), 0)
    cc = lax.broadcasted_iota(jnp.int32, (L, L), 1)
    causal = cc <= rr
    lower = causal.astype(F32)

    for h in range(H):
        q = qk[:, h * dh:(h + 1) * dh]
        k = qk[:, W + h * dh:W + (h + 1) * dh] * (dh ** -0.5)
        v = v_ref[:, h * dh:(h + 1) * dh].astype(F32)
        li_c = gcol[:, h:h + 1]
        lf_c = _log_sigmoid(gcol[:, H + h:H + h + 1])
        li_r = grow[h:h + 1, :]
        lf_r = _log_sigmoid(grow[H + h:H + h + 1, :])
        b_c = jnp.sum(lower * lf_r, axis=1, keepdims=True)
        b_r = jnp.sum((rr <= cc).astype(F32) * lf_c, axis=0, keepdims=True)
        m_prev = m_ref[h:h + 1, 0:1]

        dm = jnp.where(causal, b_c - b_r + li_r, -jnp.inf)
        inter = b_c + m_prev
        mt = jnp.maximum(inter, jnp.max(dm, axis=1, keepdims=True))
        dw = jnp.exp(dm - mt)
        iw = jnp.exp(inter - mt)
        qb = q.astype(BF16)
        kb = k.astype(BF16)
        wm = lax.dot_general(qb, kb, _NT, preferred_element_type=F32) * dw
        ct = ct_ref[h]
        n_row = n_ref[h:h + 1, :]
        num = (iw * jnp.dot(qb, ct.astype(BF16), preferred_element_type=F32)
               + jnp.dot(wm.astype(BF16), v.astype(BF16), preferred_element_type=F32))
        den = iw * jnp.sum(q * n_row, axis=1, keepdims=True) + jnp.sum(wm, axis=1, keepdims=True)
        hh = num / jnp.maximum(jnp.abs(den), jnp.exp(-mt))

        b_last = b_c[L - 1:L, :]
        g_r = b_last - b_r + li_r
        g_c = b_last - b_c + li_c
        m_new = jnp.maximum(b_last + m_prev, jnp.max(g_r, axis=1, keepdims=True))
        a = jnp.exp(b_last + m_prev - m_new)
        w_c = jnp.exp(g_c - m_new)
        ct_ref[h] = a * ct + lax.dot_general(kb, (v * w_c).astype(BF16), _TN,
                                             preferred_element_type=F32)
        n_ref[h:h + 1, :] = a * n_row + jnp.sum(k * w_c, axis=0, keepdims=True)
        m_ref[h:h + 1, :] = jnp.broadcast_to(m_new, (1, LANES))

        hn = hh * lax.rsqrt(jnp.mean(hh * hh, axis=1, keepdims=True) + EPS)
        o = hn * ng_ref[:, h * dh:(h + 1) * dh] * jax.nn.sigmoid(og_ref[:, h * dh:(h + 1) * dh])
        out_ref[:, h * dh:(h + 1) * dh] = o.astype(out_ref.dtype)


def _mlstm(m_qk, m_v, m_o, small, grow, brow, bcol, conv_w, conv_b, norm_g):
    B, S, W2 = m_qk.shape
    W = W2 // 2
    H, L = M_HEADS, M_CHUNK
    dh = W // H
    nc = S // L
    return pl.pallas_call(
        functools.partial(_mlstm_kernel, L=L, H=H, dh=dh),
        grid=(B, nc),
        in_specs=[pl.BlockSpec((None, L, W2), lambda b, c: (b, c, 0)),
                  pl.BlockSpec((None, L, W), lambda b, c: (b, c, 0)),
                  pl.BlockSpec((None, L, W), lambda b, c: (b, c, 0)),
                  pl.BlockSpec((None, L, LANES), lambda b, c: (b, c, 0)),
                  pl.BlockSpec((None, None, 8, L), lambda b, c: (b, c, 0, 0)),
                  pl.BlockSpec((1, LANES), lambda b, c: (0, 0)),
                  pl.BlockSpec((8, 1), lambda b, c: (0, 0)),
                  pl.BlockSpec((M_CONV, W2), lambda b, c: (0, 0)),
                  pl.BlockSpec((1, W2), lambda b, c: (0, 0)),
                  pl.BlockSpec((1, W), lambda b, c: (0, 0))],
        out_specs=pl.BlockSpec((None, L, W), lambda b, c: (b, c, 0)),
        out_shape=jax.ShapeDtypeStruct((B, S, W), BF16),
        scratch_shapes=[pltpu.VMEM((8 + L, W2), F32),
                        pltpu.VMEM((H, dh, dh), F32),
                        pltpu.VMEM((8, dh), F32),
                        pltpu.VMEM((8, LANES), F32)],
        compiler_params=_cparams(2),
        name="mlstm",
    )(m_qk, m_v, m_o, small, grow, brow, bcol, conv_w, conv_b, norm_g)


def _rope(x, cos, sin_signed):
    return x * cos + pltpu.roll(x, N_HEAD_DIM // 2, axis=1) * sin_signed


def _nsa_prep_kernel(q_ref, ks_ref, kw_ref, cos_ref, sin_ref, qs_out, qr_out, ks_out, kw_out):
    cos = cos_ref[...]
    sin = sin_ref[...]
    scale = N_HEAD_DIM ** -0.5
    d = N_HEAD_DIM
    for h in range(N_Q_HEADS):
        x = q_ref[:, h * d:(h + 1) * d] * scale
        qs_out[:, h * d:(h + 1) * d] = x.astype(BF16)
        qr_out[:, h * d:(h + 1) * d] = _rope(x, cos, sin).astype(BF16)
    for g in range(N_KV_GROUPS):
        ks_out[:, g * d:(g + 1) * d] = _rope(ks_ref[:, g * d:(g + 1) * d], cos, sin).astype(BF16)
        kw_out[:, g * d:(g + 1) * d] = _rope(kw_ref[:, g * d:(g + 1) * d], cos, sin).astype(BF16)


def _nsa_prep(n_q, n_ks, n_kw, cos, sin_signed, ts):
    B, S, NW = n_q.shape
    KW = n_ks.shape[-1]
    tok = lambda w: pl.BlockSpec((None, ts, w), lambda b, i: (b, i, 0))
    tab = pl.BlockSpec((ts, N_HEAD_DIM), lambda b, i: (i, 0))
    return pl.pallas_call(
        _nsa_prep_kernel,
        grid=(B, S // ts),
        in_specs=[tok(NW), tok(KW), tok(KW), tab, tab],
        out_specs=[tok(NW), tok(NW), tok(KW), tok(KW)],
        out_shape=[jax.ShapeDtypeStruct((B, S, NW), BF16), jax.ShapeDtypeStruct((B, S, NW), BF16),
                   jax.ShapeDtypeStruct((B, S, KW), BF16), jax.ShapeDtypeStruct((B, S, KW), BF16)],
        compiler_params=_cparams(2),
        name="nsa_prep",
    )(n_q, n_ks, n_kw, cos, sin_signed)


def _compress_kernel(x_ref, p_ref, wh_ref, w2_ref, o_ref):
    x = x_ref[...]
    ns = x.shape[0]
    y0 = jnp.dot((x + p_ref[0:1, :]).astype(BF16), wh_ref[0], preferred_element_type=F32)
    y1 = jnp.dot((x + p_ref[1:2, :]).astype(BF16), wh_ref[1], preferred_element_type=F32)
    act = _gelu(y0 + pltpu.roll(y1, ns - 1, axis=0))
    d = N_HEAD_DIM
    for g in range(N_KV_GROUPS):
        o_ref[:, g * d:(g + 1) * d] = jnp.dot(act[:, g * d:(g + 1) * d].astype(BF16), w2_ref[...],
                                              preferred_element_type=F32).astype(o_ref.dtype)


def _compress(x_seg, pos2, wh, w2):
    B, NS, SW = x_seg.shape
    GW = N_KV_GROUPS * N_HEAD_DIM
    return pl.pallas_call(
        _compress_kernel,
        grid=(B,),
        in_specs=[pl.BlockSpec((None, NS, SW), lambda b: (b, 0, 0)),
                  pl.BlockSpec((2, SW), lambda b: (0, 0)),
                  pl.BlockSpec((2, SW, GW), lambda b: (0, 0, 0)),
                  pl.BlockSpec((N_HEAD_DIM, N_HEAD_DIM), lambda b: (0, 0))],
        out_specs=pl.BlockSpec((None, NS, GW), lambda b: (b, 0, 0)),
        out_shape=jax.ShapeDtypeStruct((B, NS, GW), BF16),
        compiler_params=_cparams(1),
        name="nsa_compress",
    )(x_seg, pos2, wh, w2)


def _nsa_attn_kernel(qs_ref, qr_ref, kc_ref, vct_ref, ks_ref, vst_ref, kw_ref, vwt_ref, gate_ref,
                     ovt_ref, ext_ref, o_ref, acc_ref, *, TQ, TK):
    i = pl.program_id(2)
    t0 = i * TQ
    d = N_HEAD_DIM
    NC = kc_ref.shape[0]
    J = ovt_ref.shape[0]
    NQ = N_HPG * TQ
    cols = [slice(hh * TQ, (hh + 1) * TQ) for hh in range(N_HPG)]

    def stack_heads(ref):
        return jnp.concatenate([ref[:, hh * d:(hh + 1) * d] for hh in range(N_HPG)], axis=0)

    s = lax.dot_general(kc_ref[...], stack_heads(qs_ref), _NT, preferred_element_type=F32)
    nrow = lax.broadcasted_iota(jnp.int32, (NC, TQ), 0)
    tcol = t0 + lax.broadcasted_iota(jnp.int32, (NC, TQ), 1)
    cmask = (nrow * CMP_STRIDE + (CMP_BLOCK - 1)) <= tcol
    psum = jnp.zeros((NC, TQ), F32)
    ps = []
    for hh in range(N_HPG):
        sh = jnp.where(cmask, s[:, cols[hh]], NEG)
        p = jnp.where(cmask, jnp.exp(sh - jnp.max(sh, axis=0, keepdims=True)), 0.0)
        l = jnp.sum(p, axis=0, keepdims=True)
        p = p * (1.0 / jnp.where(l > 0.0, l, 1.0))
        psum = psum + p
        ps.append(p.astype(BF16))
    o_cmp = jnp.dot(vct_ref[...], jnp.concatenate(ps, axis=1), preferred_element_type=F32)

    imp = jnp.dot(ovt_ref[...], psum, preferred_element_type=F32, precision=lax.Precision.HIGHEST)
    jrow = lax.broadcasted_iota(jnp.int32, (J, TQ), 0)
    tj = t0 + lax.broadcasted_iota(jnp.int32, (J, TQ), 1)
    cur = jnp.right_shift(tj, SLC_BLOCK.bit_length() - 1)
    forced = (jrow == 0) | (jrow == cur) | (jrow == cur - 1)
    valid = (jrow * SLC_BLOCK) <= tj
    score = jnp.where(forced, 1e9, jnp.where(valid, imp, -1e9))
    sel = jnp.zeros((J, TQ), F32)
    for _ in range(SLC_TOP):
        mx = jnp.max(score, axis=0, keepdims=True)
        idx = jnp.min(jnp.where(score == mx, jrow, 1 << 30), axis=0, keepdims=True)
        hit = jrow == idx
        sel = jnp.where(hit & (mx > -1e8), 1.0, sel)
        score = jnp.where(hit, -3e38, score)
    selb = sel.astype(BF16)

    qr = stack_heads(qr_ref)
    acc_ref[...] = jnp.zeros(acc_ref.shape, F32)
    krow = lax.broadcasted_iota(jnp.int32, (TK, TQ), 0)
    tk_col = t0 + lax.broadcasted_iota(jnp.int32, (TK, TQ), 1)

    def sweep(kt, carry):
        m, l = carry
        k0 = pl.multiple_of(kt * TK, TK)
        st = lax.dot_general(ks_ref[pl.ds(k0, TK), :], qr, _NT, preferred_element_type=F32)
        bm = jnp.dot(ext_ref[kt], selb, preferred_element_type=F32)
        mask = (bm > 0.5) & ((k0 + krow) <= tk_col)
        ms, ls, als, pts = [], [], [], []
        for hh in range(N_HPG):
            sh = jnp.where(mask, st[:, cols[hh]], NEG)
            m_prev = m[:, cols[hh]]
            m_new = jnp.maximum(m_prev, jnp.max(sh, axis=0, keepdims=True))
            p = jnp.exp(sh - m_new)
            al = jnp.exp(m_prev - m_new)
            ms.append(m_new)
            als.append(al)
            ls.append(al * l[:, cols[hh]] + jnp.sum(p, axis=0, keepdims=True))
            pts.append(p.astype(BF16))
        acc_ref[...] = (acc_ref[...] * jnp.concatenate(als, axis=1)
                        + jnp.dot(vst_ref[kt], jnp.concatenate(pts, axis=1),
                                  preferred_element_type=F32))
        return jnp.concatenate(ms, axis=1), jnp.concatenate(ls, axis=1)

    m0 = jnp.full((1, NQ), NEG, F32)
    l0 = jnp.zeros((1, NQ), F32)
    _, l_s = lax.fori_loop(0, (t0 + TQ + TK - 1) // TK, sweep, (m0, l0))
    o_slc = acc_ref[...] * (1.0 / jnp.where(l_s > 0.0, l_s, 1.0))

    span = WINDOW + TQ
    wt0 = jnp.maximum(i - WINDOW // TQ, 0)
    ws = pl.multiple_of(wt0 * TQ, TQ)
    sw = lax.dot_general(kw_ref[pl.ds(ws, span), :], qr, _NT, preferred_element_type=F32)
    kp = ws + lax.broadcasted_iota(jnp.int32, (span, TQ), 0)
    tq_w = t0 + lax.broadcasted_iota(jnp.int32, (span, TQ), 1)
    wmask = (kp <= tq_w) & (kp > tq_w - WINDOW)
    pws, lws = [], []
    for hh in range(N_HPG):
        sh = jnp.where(wmask, sw[:, cols[hh]], NEG)
        p = jnp.exp(sh - jnp.max(sh, axis=0, keepdims=True))
        lws.append(jnp.sum(p, axis=0, keepdims=True))
        pws.append(p.astype(BF16))
    pw = jnp.concatenate(pws, axis=1)
    o_win = jnp.dot(vwt_ref[wt0], pw[0:TQ, :], preferred_element_type=F32)
    for j in range(1, span // TQ):
        o_win = o_win + jnp.dot(vwt_ref[wt0 + j], pw[j * TQ:(j + 1) * TQ, :],
                                preferred_element_type=F32)
    o_win = o_win * (1.0 / jnp.concatenate(lws, axis=1))

    gt = jax.nn.sigmoid(gate_ref[...]).T
    for hh in range(N_HPG):
        o = (gt[3 * hh:3 * hh + 1, :] * o_cmp[:, cols[hh]]
             + gt[3 * hh + 1:3 * hh + 2, :] * o_slc[:, cols[hh]]
             + gt[3 * hh + 2:3 * hh + 3, :] * o_win[:, cols[hh]])
        o_ref[:, hh * d:(hh + 1) * d] = o.T.astype(o_ref.dtype)


def _nsa_attn(q_s, q_r, kcmp, vcmp_t, ks_r, vs_t, kw_r, vw_t, gates, overlap_t, expand_t):
    B, S, _ = q_s.shape
    G, d, TQ = N_KV_GROUPS, N_HEAD_DIM, Q_BLOCK
    NC = kcmp.shape[1]
    TK = vs_t.shape[-1]
    gw = N_HPG * d
    qspec = pl.BlockSpec((None, TQ, gw), lambda b, g, i: (b, i, g))
    kspec = pl.BlockSpec((None, S, d), lambda b, g, i: (b, 0, g))
    tspec = lambda a: pl.BlockSpec((None, None) + a.shape[2:],
                                   lambda b, g, i: (b, g) + (0,) * (a.ndim - 2))
    const = lambda a: pl.BlockSpec(a.shape, lambda b, g, i: (0,) * a.ndim)
    return pl.pallas_call(
        functools.partial(_nsa_attn_kernel, TQ=TQ, TK=TK),
        grid=(B, G, S // TQ),
        in_specs=[qspec, qspec,
                  pl.BlockSpec((None, NC, d), lambda b, g, i: (b, 0, g)), tspec(vcmp_t),
                  kspec, tspec(vs_t), kspec, tspec(vw_t),
                  pl.BlockSpec((None, TQ, LANES), lambda b, g, i: (b, i, g)),
                  const(overlap_t), const(expand_t)],
        out_specs=qspec,
        out_shape=jax.ShapeDtypeStruct((B, S, G * gw), BF16),
        scratch_shapes=[pltpu.VMEM((d, N_HPG * TQ), F32)],
        compiler_params=_cparams(3),
        name="nsa_attn",
    )(q_s, q_r, kcmp, vcmp_t, ks_r, vs_t, kw_r, vw_t, gates, overlap_t, expand_t)


def _mix_kernel(hm_ref, on_ref, ga_ref, gb_ref, x_ref, wm_ref, wn_ref, wo_ref, g2_ref,
                h1_ref, xn2_ref):
    ya = jnp.dot(hm_ref[...], wm_ref[...], preferred_element_type=F32)
    yb = jnp.dot(on_ref[...], wn_ref[...], preferred_element_type=F32)
    mix = jax.nn.sigmoid(ga_ref[...]) * ya + jax.nn.sigmoid(gb_ref[...]) * yb
    h1 = x_ref[...] + jnp.dot(mix.astype(BF16), wo_ref[...], preferred_element_type=F32)
    h1_ref[...] = h1
    xn2_ref[...] = _rms(h1, g2_ref[...]).astype(BF16)


def _mix(hm, on, g_a, g_b, x2d, w_m, w_n, w_o, g2, tm):
    T, D = x2d.shape
    tok = pl.BlockSpec((tm, D), lambda i: (i, 0))
    wsp = pl.BlockSpec((D, D), lambda i: (0, 0))
    return pl.pallas_call(
        _mix_kernel,
        grid=(T // tm,),
        in_specs=[tok, tok, tok, tok, tok, wsp, wsp, wsp, pl.BlockSpec((1, D), lambda i: (0, 0))],
        out_specs=[tok, tok],
        out_shape=[jax.ShapeDtypeStruct((T, D), F32), jax.ShapeDtypeStruct((T, D), BF16)],
        compiler_params=_cparams(1),
        name="mix",
    )(hm, on, g_a, g_b, x2d, w_m, w_n, w_o, g2.reshape(1, D))


def _top_values(s, k):
    vals = []
    for _ in range(k):
        m = jnp.max(s, axis=0, keepdims=True)
        vals.append(m)
        s = jnp.where(s == m, -jnp.inf, s)
    return vals


def _peer_score_kernel(xn_ref, wq_ref, k1_ref, k2_ref, a_ref, c_ref, s2_ref, e2_ref):
    q = jnp.dot(xn_ref[...], wq_ref[...], preferred_element_type=F32)
    nk = P_KEYS
    for h in range(P_HEADS):
        qh = q[:, h * 2 * P_HALF:(h + 1) * 2 * P_HALF].astype(BF16)
        s1 = lax.dot_general(k1_ref[...], qh, _NT, preferred_element_type=F32)
        s2 = lax.dot_general(k2_ref[...], qh, _NT, preferred_element_type=F32)
        v1 = _top_values(s1, P_TOPK)
        v2 = _top_values(s2, P_TOPK)
        v2m = jnp.concatenate(v2, axis=0)
        cand = jnp.concatenate([v1[a] + v2m for a in range(P_TOPK)], axis=0)
        top = _top_values(cand, P_TOPK)
        z = jnp.exp(top[0] - top[0])
        for t in top[1:]:
            z = z + jnp.exp(t - top[0])
        cut = jnp.full(s1.shape, jnp.inf, F32)
        for b in range(P_TOPK):
            cut = jnp.where(s1 + v2[b] >= top[-1], jnp.minimum(cut, v2[b]), cut)
        rows = slice(h * nk, (h + 1) * nk)
        a_ref[rows, :] = cut
        c_ref[rows, :] = jnp.exp(s1 - v1[0]) / z
        s2_ref[rows, :] = s2
        e2_ref[rows, :] = jnp.exp(s2 - v2[0])


def _peer_score(xn2, wq, k1p, k2p, tt):
    T, D = xn2.shape
    nt = T // tt
    R = P_HEADS * P_KEYS
    osp = pl.BlockSpec((None, R, tt), lambda i: (i, 0, 0))
    osh = jax.ShapeDtypeStruct((nt, R, tt), F32)
    return pl.pallas_call(
        _peer_score_kernel,
        grid=(nt,),
        in_specs=[pl.BlockSpec((tt, D), lambda i: (i, 0)),
                  pl.BlockSpec(wq.shape, lambda i: (0, 0)),
                  pl.BlockSpec(k1p.shape, lambda i: (0, 0)),
                  pl.BlockSpec(k2p.shape, lambda i: (0, 0))],
        out_specs=[osp, osp, osp, osp],
        out_shape=[osh, osh, osh, osh],
        compiler_params=_cparams(1),
        name="peer_score",
    )(xn2, wq, k1p, k2p)


PEER_ROWS = 32
PEER_COLS = 128
PEER_CHUNKS = 2


def _routing_block(a_ref, c_ref, s2_ref, e2_ref, act_ref, p_ref, i1_base, n_i1):
    nk = P_KEYS
    tt = act_ref.shape[1]
    for il in range(n_i1):
        cut_rows = [a_ref[pl.ds(h * nk + i1_base + il, 1), :] for h in range(P_HEADS)]
        coef_rows = [c_ref[pl.ds(h * nk + i1_base + il, 1), :] for h in range(P_HEADS)]
        for lb in range(tt // PEER_COLS):
            lanes = slice(lb * PEER_COLS, (lb + 1) * PEER_COLS)
            blk = (PEER_ROWS, PEER_COLS)
            cut = [jnp.broadcast_to(r[:, lanes], blk) for r in cut_rows]
            coef = [jnp.broadcast_to(r[:, lanes], blk) for r in coef_rows]
            for rb in range(nk // PEER_ROWS):
                w = None
                for h in range(P_HEADS):
                    rows = slice(h * nk + rb * PEER_ROWS, h * nk + (rb + 1) * PEER_ROWS)
                    t = jnp.where(s2_ref[rows, lanes] >= cut[h], coef[h] * e2_ref[rows, lanes], 0.0)
                    w = t if w is None else w + t
                dst = slice(il * nk + rb * PEER_ROWS, il * nk + (rb + 1) * PEER_ROWS)
                p_ref[dst, lanes] = (w * _gelu(act_ref[dst, lanes])).astype(BF16)


def _peer_dense_kernel(xn_ref, a_ref, c_ref, s2_ref, e2_ref, u_ref, vt_ref, h1_ref, gf_ref,
                       out_ref, *scratch, EC):
    acts = scratch[:PEER_CHUNKS]
    ps = scratch[PEER_CHUNKS:2 * PEER_CHUNKS]
    acc_ref = scratch[2 * PEER_CHUNKS]
    ci = pl.program_id(1)
    n_i1 = EC // P_KEYS

    @pl.when(ci == 0)
    def _init():
        acc_ref[...] = jnp.zeros_like(acc_ref)

    xn = xn_ref[...]
    for j in range(PEER_CHUNKS):
        acts[j][...] = lax.dot_general(u_ref[j * EC:(j + 1) * EC, :], xn, _NT,
                                       preferred_element_type=F32)
    for j in range(PEER_CHUNKS):
        _routing_block(a_ref, c_ref, s2_ref, e2_ref, acts[j], ps[j],
                       (ci * PEER_CHUNKS + j) * n_i1, n_i1)
        acc_ref[...] += jnp.dot(vt_ref[:, j * EC:(j + 1) * EC], ps[j][...],
                                preferred_element_type=F32)

    @pl.when(ci == pl.num_programs(1) - 1)
    def _finish():
        y = h1_ref[...] + acc_ref[...].T
        out_ref[...] = _rms(y, gf_ref[...])


def _peer_dense(xn2, a, c, s2, e2, u, vt, h1, gf, tt, ec):
    T, D = xn2.shape
    E = u.shape[0]
    R = P_HEADS * P_KEYS
    step = PEER_CHUNKS * ec
    ssp = pl.BlockSpec((None, R, tt), lambda t, k: (t, 0, 0))
    tok = pl.BlockSpec((tt, D), lambda t, k: (t, 0))
    return pl.pallas_call(
        functools.partial(_peer_dense_kernel, EC=ec),
        grid=(T // tt, E // step),
        in_specs=[tok, ssp, ssp, ssp, ssp,
                  pl.BlockSpec((step, D), lambda t, k: (k, 0)),
                  pl.BlockSpec((D, step), lambda t, k: (0, k)),
                  pl.BlockSpec((tt, D), lambda t, k: (t, 0), pipeline_mode=pl.Buffered(1)),
                  pl.BlockSpec((1, D), lambda t, k: (0, 0))],
        out_specs=tok,
        out_shape=jax.ShapeDtypeStruct((T, D), F32),
        scratch_shapes=([pltpu.VMEM((ec, tt), F32)] * PEER_CHUNKS
                        + [pltpu.VMEM((ec, tt), BF16)] * PEER_CHUNKS
                        + [pltpu.VMEM((D, tt), F32)]),
        compiler_params=_cparams(2),
        name="peer_dense",
    )(xn2, a, c, s2, e2, u, vt, h1, gf.reshape(1, D))


def _pad_cols(w, width):
    return jnp.pad(w, ((0, 0), (0, width - w.shape[1])))


def _segment_weights(w1):
    d, G = N_HEAD_DIM, N_KV_GROUPS
    half = CMP_BLOCK // 2
    w = w1.reshape(2, half, d, d)
    eye = jnp.eye(G, dtype=w1.dtype)
    w = w[:, :, None, :, None, :] * eye[None, None, :, None, :, None]
    return w.reshape(2, half * G * d, G * d)


def _segment_pos(pos):
    d, G = N_HEAD_DIM, N_KV_GROUPS
    half = CMP_BLOCK // 2
    p = jnp.broadcast_to(pos.reshape(2, half, 1, d), (2, half, G, d))
    return p.reshape(2, half * G * d)


def _values_t(v, tile):
    B, S, _ = v.shape
    v = v.reshape(B, S // tile, tile, N_KV_GROUPS, N_HEAD_DIM)
    return v.transpose(0, 3, 1, 4, 2)


def _forward(x, ln_mix_g, w_in, m_conv_w, m_conv_b, m_i_bias, m_f_bias, m_norm_g, w_m_out, cmp_k_pos, cmp_k_w1, cmp_k_w2, cmp_v_pos, cmp_v_w1, cmp_v_w2, w_n_out, w_out, ln_ffn_g, peer_wq, peer_k1, peer_k2, peer_u, peer_v, ln_f_g):
    B, S, D = x.shape
    T = B * S
    assert ln_mix_g.shape[0] == 1, "the final norm is fused into the last stage of a single layer"
    assert S % Q_BLOCK == 0 and S >= WINDOW + Q_BLOCK and S // SLC_BLOCK >= SLC_TOP
    l = 0
    d, G, H = N_HEAD_DIM, N_KV_GROUPS, M_HEADS
    MW = D
    KVW = G * d
    h = x.reshape(T, D)

    inv = ROPE_THETA ** (-jnp.arange(0, d, 2, dtype=F32) / d)
    ang = jnp.arange(S, dtype=F32)[:, None] * inv[None, :]
    ang = jnp.concatenate([ang, ang], axis=-1)
    cos = jnp.cos(ang)
    sin_signed = jnp.sin(ang) * jnp.concatenate([-jnp.ones((d // 2,), F32), jnp.ones((d // 2,), F32)])
    ncp = S // CMP_STRIDE
    n_slc = S // SLC_BLOCK
    n_idx = jnp.arange(ncp)[None, :] * CMP_STRIDE
    j_idx = jnp.arange(n_slc)[:, None] * SLC_BLOCK
    overlap_t = ((n_idx < j_idx + SLC_BLOCK) & (n_idx + CMP_BLOCK > j_idx)).astype(F32)
    tk = min(512, S)
    expand_t = (jnp.arange(S).reshape(S // tk, tk, 1) // SLC_BLOCK
                == jnp.arange(n_slc)[None, None, :]).astype(BF16)

    wl = w_in[l]
    offs = [0]
    for wdt in (2 * MW, MW, MW, H, H, D, KVW, KVW, KVW, KVW, KVW, KVW, 3 * N_Q_HEADS, D, D):
        offs.append(offs[-1] + wdt)
    col = lambda i: wl[:, offs[i]:offs[i + 1]]
    w_small = _pad_cols(jnp.concatenate([col(3), col(4)], axis=1), LANES)
    w_ng = col(12).reshape(D, G, 3 * N_HPG)
    w_ng = jnp.pad(w_ng, ((0, 0), (0, 0), (0, LANES - 3 * N_HPG))).reshape(D, G * LANES)
    groups = [(col(0), F32), (col(1), BF16), (col(2), F32), (col(5), F32),
              (col(6), F32), (col(7), F32), (col(8), F32), (col(9), BF16),
              (col(10), F32), (col(11), BF16), (col(13), F32), (col(14), F32),
              (w_small, F32), (w_ng, F32)]
    w_cat = jnp.concatenate([g[0] for g in groups], axis=1).astype(BF16)
    (m_qk, m_v, m_o, n_q, n_kc, n_vc, n_ks, n_vs, n_kw, n_vw, g_a, g_b, small, n_g) = _in_proj(
        h, ln_mix_g[l], w_cat, [g[0].shape[1] for g in groups], [g[1] for g in groups], tm=256)

    small3 = small.reshape(B, S, LANES)
    grow = small3[:, :, :8].reshape(B, S // M_CHUNK, M_CHUNK, 8).transpose(0, 1, 3, 2)
    bias = jnp.concatenate([m_i_bias[l], m_f_bias[l]])
    hm = _mlstm(m_qk.reshape(B, S, 2 * MW), m_v.reshape(B, S, MW), m_o.reshape(B, S, MW),
                small3, grow, _pad_cols(bias.reshape(1, 2 * H), LANES), bias.reshape(2 * H, 1),
                m_conv_w[l], m_conv_b[l].reshape(1, 2 * MW), m_norm_g[l].reshape(1, MW))

    q_s, q_r, ks_r, kw_r = _nsa_prep(n_q.reshape(B, S, D), n_ks.reshape(B, S, KVW),
                                     n_kw.reshape(B, S, KVW), cos, sin_signed, ts=min(512, S))
    seg = CMP_STRIDE * KVW
    kcmp = _compress(n_kc.reshape(B, ncp, seg), _segment_pos(cmp_k_pos[l]),
                     _segment_weights(cmp_k_w1[l]).astype(BF16), cmp_k_w2[l].astype(BF16))
    vcmp = _compress(n_vc.reshape(B, ncp, seg), _segment_pos(cmp_v_pos[l]),
                     _segment_weights(cmp_v_w1[l]).astype(BF16), cmp_v_w2[l].astype(BF16))
    vcmp_t = vcmp.reshape(B, ncp, G, d).transpose(0, 2, 3, 1)
    on = _nsa_attn(q_s, q_r, kcmp, vcmp_t, ks_r, _values_t(n_vs.reshape(B, S, KVW), tk), kw_r,
                   _values_t(n_vw.reshape(B, S, KVW), Q_BLOCK), n_g.reshape(B, S, G * LANES),
                   overlap_t, expand_t)

    h1, xn2 = _mix(hm.reshape(T, MW), on.reshape(T, D), g_a, g_b, h,
                   w_m_out[l].astype(BF16), w_n_out[l].astype(BF16), w_out[l].astype(BF16),
                   ln_ffn_g[l], tm=256)

    tt = min(512, T)
    k1p = jnp.pad(peer_k1[l], ((0, 0), (0, P_HALF))).astype(BF16)
    k2p = jnp.pad(peer_k2[l], ((0, 0), (P_HALF, 0))).astype(BF16)
    a, c, s2, e2 = _peer_score(xn2, peer_wq[l].astype(BF16), k1p, k2p, tt)
    out = _peer_dense(xn2, a, c, s2, e2, peer_u[l].astype(BF16), peer_v[l].T.astype(BF16),
                      h1, ln_f_g, tt, ec=1024)
    aux = dict(m_qk=m_qk, m_v=m_v, small=small, hm=hm, q_s=q_s, q_r=q_r, ks_r=ks_r, kw_r=kw_r,
               kcmp=kcmp, vcmp=vcmp, on=on, h1=h1, xn2=xn2, a=a, c=c, s2=s2, e2=e2)
    return out.reshape(B, S, D), aux


def kernel(x, ln_mix_g, w_in, m_conv_w, m_conv_b, m_i_bias, m_f_bias, m_norm_g, w_m_out, cmp_k_pos, cmp_k_w1, cmp_k_w2, cmp_v_pos, cmp_v_w1, cmp_v_w2, w_n_out, w_out, ln_ffn_g, peer_wq, peer_k1, peer_k2, peer_u, peer_v, ln_f_g):
    return _forward(x, ln_mix_g, w_in, m_conv_w, m_conv_b, m_i_bias, m_f_bias, m_norm_g, w_m_out,
                    cmp_k_pos, cmp_k_w1, cmp_k_w2, cmp_v_pos, cmp_v_w1, cmp_v_w2, w_n_out, w_out,
                    ln_ffn_g, peer_wq, peer_k1, peer_k2, peer_u, peer_v, ln_f_g)[0]
```

```python
import functools

import jax
import jax.numpy as jnp
from jax import lax
from jax.experimental import pallas as pl
from jax.experimental.pallas import tpu as pltpu

F32 = jnp.float32
BF16 = jnp.bfloat16

EPS = 1e-6
M_HEADS = 4
M_CONV = 4
M_CHUNK = 64
N_Q_HEADS = 8
N_KV_GROUPS = 2
N_HPG = N_Q_HEADS // N_KV_GROUPS
N_HEAD_DIM = 128
CMP_BLOCK = 32
CMP_STRIDE = 16
SLC_BLOCK = 64
SLC_TOP = 8
WINDOW = 512
Q_BLOCK = 128
ROPE_THETA = 10000.0
P_HEADS = 8
P_KEYS = 128
P_HALF = 64
P_TOPK = 16

LANES = 128
VMEM_LIMIT = 56 * 1024 * 1024
NEG = -1e30

_NT = (((1,), (1,)), ((), ()))
_TN = (((0,), (0,)), ((), ()))


def _cparams(n_axes, flags=None):
    return pltpu.CompilerParams(dimension_semantics=("arbitrary",) * n_axes,
                                vmem_limit_bytes=VMEM_LIMIT, flags=flags)


def _rms(x, g):
    return x * lax.rsqrt(jnp.mean(x * x, axis=-1, keepdims=True) + EPS) * g


def _gelu(x):
    return 0.5 * x * (1.0 + lax.erf(x * (2.0 ** -0.5)))


def _in_proj_kernel(x_ref, g_ref, w_ref, *o_refs, widths):
    xn = _rms(x_ref[...], g_ref[...]).astype(BF16)
    off = 0
    for o_ref, w in zip(o_refs, widths):
        o_ref[...] = jnp.dot(xn, w_ref[:, off:off + w],
                             preferred_element_type=F32).astype(o_ref.dtype)
        off += w


def _in_proj(x2d, g, w, widths, dtypes, tm):
    T, D = x2d.shape
    N = w.shape[1]
    return pl.pallas_call(
        functools.partial(_in_proj_kernel, widths=tuple(widths)),
        grid=(T // tm,),
        in_specs=[pl.BlockSpec((tm, D), lambda i: (i, 0)),
                  pl.BlockSpec((1, D), lambda i: (0, 0)),
                  pl.BlockSpec((D, N), lambda i: (0, 0), pipeline_mode=pl.Buffered(1))],
        out_specs=[pl.BlockSpec((tm, wd), lambda i: (i, 0)) for wd in widths],
        out_shape=[jax.ShapeDtypeStruct((T, wd), dt) for wd, dt in zip(widths, dtypes)],
        compiler_params=_cparams(1),
        name="in_proj",
    )(x2d, g.reshape(1, D), w)


def _log_sigmoid(x):
    return jnp.minimum(x, 0.0) - jnp.log1p(jnp.exp(-jnp.abs(x)))


def _mlstm_kernel(qk_ref, v_ref, og_ref, gcol_ref, grow_ref, brow_ref, bcol_ref, cw_ref, cb_ref,
                  ng_ref, out_ref, ext_ref, ct_ref, n_ref, m_ref, *, L, H, dh):
    W = H * dh

    @pl.when(pl.program_id(1) == 0)
    def _init():
        ext_ref[0:8, :] = jnp.zeros((8, 2 * W), F32)
        ct_ref[...] = jnp.zeros_like(ct_ref)
        n_ref[...] = jnp.zeros_like(n_ref)
        m_ref[...] = jnp.zeros_like(m_ref)

    x = qk_ref[...]
    ext_ref[8:8 + L, :] = x
    pre = cb_ref[...] + cw_ref[0:1, :] * ext_ref[8 - (M_CONV - 1):8 - (M_CONV - 1) + L, :]
    for j in range(1, M_CONV):
        o = 8 - (M_CONV - 1) + j
        pre = pre + cw_ref[j:j + 1, :] * ext_ref[o:o + L, :]
    ext_ref[0:8, :] = x[L - 8:L, :]
    qk = pre * jax.nn.sigmoid(pre)

    gcol = gcol_ref[...] + brow_ref[...]
    grow = grow_ref[...] + bcol_ref[...]
    rr = lax.broadcasted_iota(jnp.int32, (L, L), 0)
    cc = lax.broadcasted_iota(jnp.int32, (L, L), 1)
    causal = cc <= rr
    lower = causal.astype(F32)

    for h in range(H):
        q = qk[:, h * dh:(h + 1) * dh]
        k = qk[:, W + h * dh:W + (h + 1) * dh] * (dh ** -0.5)
        v = v_ref[:, h * dh:(h + 1) * dh].astype(F32)
        li_c = gcol[:, h:h + 1]
        lf_c = _log_sigmoid(gcol[:, H + h:H + h + 1])
        li_r = grow[h:h + 1, :]
        lf_r = _log_sigmoid(grow[H + h:H + h + 1, :])
        b_c = jnp.sum(lower * lf_r, axis=1, keepdims=True)
        b_r = jnp.sum((rr <= cc).astype(F32) * lf_c, axis=0, keepdims=True)
        m_prev = m_ref[h:h + 1, 0:1]

        dm = jnp.where(causal, b_c - b_r + li_r, -jnp.inf)
        inter = b_c + m_prev
        mt = jnp.maximum(inter, jnp.max(dm, axis=1, keepdims=True))
        dw = jnp.exp(dm - mt)
        iw = jnp.exp(inter - mt)
        qb = q.astype(BF16)
        kb = k.astype(BF16)
        wm = lax.dot_general(qb, kb, _NT, preferred_element_type=F32) * dw
        ct = ct_ref[h]
        n_row = n_ref[h:h + 1, :]
        num = (iw * jnp.dot(qb, ct.astype(BF16), preferred_element_type=F32)
               + jnp.dot(wm.astype(BF16), v.astype(BF16), preferred_element_type=F32))
        den = iw * jnp.sum(q * n_row, axis=1, keepdims=True) + jnp.sum(wm, axis=1, keepdims=True)
        hh = num / jnp.maximum(jnp.abs(den), jnp.exp(-mt))

        b_last = b_c[L - 1:L, :]
        g_r = b_last - b_r + li_r
        g_c = b_last - b_c + li_c
        m_new = jnp.maximum(b_last + m_prev, jnp.max(g_r, axis=1, keepdims=True))
        a = jnp.exp(b_last + m_prev - m_new)
        w_c = jnp.exp(g_c - m_new)
        ct_ref[h] = a * ct + lax.dot_general(kb, (v * w_c).astype(BF16), _TN,
                                             preferred_element_type=F32)
        n_ref[h:h + 1, :] = a * n_row + jnp.sum(k * w_c, axis=0, keepdims=True)
        m_ref[h:h + 1, :] = jnp.broadcast_to(m_new, (1, LANES))

        hn = hh * lax.rsqrt(jnp.mean(hh * hh, axis=1, keepdims=True) + EPS)
        o = hn * ng_ref[:, h * dh:(h + 1) * dh] * jax.nn.sigmoid(og_ref[:, h * dh:(h + 1) * dh])
        out_ref[:, h * dh:(h + 1) * dh] = o.astype(out_ref.dtype)


def _mlstm(m_qk, m_v, m_o, small, grow, brow, bcol, conv_w, conv_b, norm_g):
    B, S, W2 = m_qk.shape
    W = W2 // 2
    H, L = M_HEADS, M_CHUNK
    dh = W // H
    nc = S // L
    return pl.pallas_call(
        functools.partial(_mlstm_kernel, L=L, H=H, dh=dh),
        grid=(B, nc),
        in_specs=[pl.BlockSpec((None, L, W2), lambda b, c: (b, c, 0)),
                  pl.BlockSpec((None, L, W), lambda b, c: (b, c, 0)),
                  pl.BlockSpec((None, L, W), lambda b, c: (b, c, 0)),
                  pl.BlockSpec((None, L, LANES), lambda b, c: (b, c, 0)),
                  pl.BlockSpec((None, None, 8, L), lambda b, c: (b, c, 0, 0)),
                  pl.BlockSpec((1, LANES), lambda b, c: (0, 0)),
                  pl.BlockSpec((8, 1), lambda b, c: (0, 0)),
                  pl.BlockSpec((M_CONV, W2), lambda b, c: (0, 0)),
                  pl.BlockSpec((1, W2), lambda b, c: (0, 0)),
                  pl.BlockSpec((1, W), lambda b, c: (0, 0))],
        out_specs=pl.BlockSpec((None, L, W), lambda b, c: (b, c, 0)),
        out_shape=jax.ShapeDtypeStruct((B, S, W), BF16),
        scratch_shapes=[pltpu.VMEM((8 + L, W2), F32),
                        pltpu.VMEM((H, dh, dh), F32),
                        pltpu.VMEM((8, dh), F32),
                        pltpu.VMEM((8, LANES), F32)],
        compiler_params=_cparams(2),
        name="mlstm",
    )(m_qk, m_v, m_o, small, grow, brow, bcol, conv_w, conv_b, norm_g)


def _rope(x, cos, sin_signed):
    return x * cos + pltpu.roll(x, N_HEAD_DIM // 2, axis=1) * sin_signed


def _nsa_prep_kernel(q_ref, ks_ref, kw_ref, cos_ref, sin_ref, qs_out, qr_out, ks_out, kw_out):
    cos = cos_ref[...]
    sin = sin_ref[...]
    scale = N_HEAD_DIM ** -0.5
    d = N_HEAD_DIM
    for h in range(N_Q_HEADS):
        x = q_ref[:, h * d:(h + 1) * d] * scale
        qs_out[:, h * d:(h + 1) * d] = x.astype(BF16)
        qr_out[:, h * d:(h + 1) * d] = _rope(x, cos, sin).astype(BF16)
    for g in range(N_KV_GROUPS):
        ks_out[:, g * d:(g + 1) * d] = _rope(ks_ref[:, g * d:(g + 1) * d], cos, sin).astype(BF16)
        kw_out[:, g * d:(g + 1) * d] = _rope(kw_ref[:, g * d:(g + 1) * d], cos, sin).astype(BF16)


def _nsa_prep(n_q, n_ks, n_kw, cos, sin_signed, ts):
    B, S, NW = n_q.shape
    KW = n_ks.shape[-1]
    tok = lambda w: pl.BlockSpec((None, ts, w), lambda b, i: (b, i, 0))
    tab = pl.BlockSpec((ts, N_HEAD_DIM), lambda b, i: (i, 0))
    return pl.pallas_call(
        _nsa_prep_kernel,
        grid=(B, S // ts),
        in_specs=[tok(NW), tok(KW), tok(KW), tab, tab],
        out_specs=[tok(NW), tok(NW), tok(KW), tok(KW)],
        out_shape=[jax.ShapeDtypeStruct((B, S, NW), BF16), jax.ShapeDtypeStruct((B, S, NW), BF16),
                   jax.ShapeDtypeStruct((B, S, KW), BF16), jax.ShapeDtypeStruct((B, S, KW), BF16)],
        compiler_params=_cparams(2),
        name="nsa_prep",
    )(n_q, n_ks, n_kw, cos, sin_signed)


def _compress_kernel(x_ref, p_ref, wh_ref, w2_ref, o_ref):
    x = x_ref[...]
    ns = x.shape[0]
    y0 = jnp.dot((x + p_ref[0:1, :]).astype(BF16), wh_ref[0], preferred_element_type=F32)
    y1 = jnp.dot((x + p_ref[1:2, :]).astype(BF16), wh_ref[1], preferred_element_type=F32)
    act = _gelu(y0 + pltpu.roll(y1, ns - 1, axis=0))
    d = N_HEAD_DIM
    for g in range(N_KV_GROUPS):
        o_ref[:, g * d:(g + 1) * d] = jnp.dot(act[:, g * d:(g + 1) * d].astype(BF16), w2_ref[...],
                                              preferred_element_type=F32).astype(o_ref.dtype)


def _compress(x_seg, pos2, wh, w2):
    B, NS, SW = x_seg.shape
    GW = N_KV_GROUPS * N_HEAD_DIM
    return pl.pallas_call(
        _compress_kernel,
        grid=(B,),
        in_specs=[pl.BlockSpec((None, NS, SW), lambda b: (b, 0, 0)),
                  pl.BlockSpec((2, SW), lambda b: (0, 0)),
                  pl.BlockSpec((2, SW, GW), lambda b: (0, 0, 0)),
                  pl.BlockSpec((N_HEAD_DIM, N_HEAD_DIM), lambda b: (0, 0))],
        out_specs=pl.BlockSpec((None, NS, GW), lambda b: (b, 0, 0)),
        out_shape=jax.ShapeDtypeStruct((B, NS, GW), BF16),
        compiler_params=_cparams(1),
        name="nsa_compress",
    )(x_seg, pos2, wh, w2)


def _nsa_attn_kernel(qs_ref, qr_ref, kc_ref, vct_ref, ks_ref, vst_ref, kw_ref, vwt_ref, gate_ref,
                     ovt_ref, ext_ref, o_ref, acc_ref, *, TQ, TK):
    i = pl.program_id(2)
    t0 = i * TQ
    d = N_HEAD_DIM
    NC = kc_ref.shape[0]
    J = ovt_ref.shape[0]
    NQ = N_HPG * TQ
    cols = [slice(hh * TQ, (hh + 1) * TQ) for hh in range(N_HPG)]

    def stack_heads(ref):
        return jnp.concatenate([ref[:, hh * d:(hh + 1) * d] for hh in range(N_HPG)], axis=0)

    s = lax.dot_general(kc_ref[...], stack_heads(qs_ref), _NT, preferred_element_type=F32)
    nrow = lax.broadcasted_iota(jnp.int32, (NC, TQ), 0)
    tcol = t0 + lax.broadcasted_iota(jnp.int32, (NC, TQ), 1)
    cmask = (nrow * CMP_STRIDE + (CMP_BLOCK - 1)) <= tcol
    psum = jnp.zeros((NC, TQ), F32)
    ps = []
    for hh in range(N_HPG):
        sh = jnp.where(cmask, s[:, cols[hh]], NEG)
        p = jnp.where(cmask, jnp.exp(sh - jnp.max(sh, axis=0, keepdims=True)), 0.0)
        l = jnp.sum(p, axis=0, keepdims=True)
        p = p * (1.0 / jnp.where(l > 0.0, l, 1.0))
        psum = psum + p
        ps.append(p.astype(BF16))
    o_cmp = jnp.dot(vct_ref[...], jnp.concatenate(ps, axis=1), preferred_element_type=F32)

    imp = jnp.dot(ovt_ref[...], psum, preferred_element_type=F32, precision=lax.Precision.HIGHEST)
    jrow = lax.broadcasted_iota(jnp.int32, (J, TQ), 0)
    tj = t0 + lax.broadcasted_iota(jnp.int32, (J, TQ), 1)
    cur = jnp.right_shift(tj, SLC_BLOCK.bit_length() - 1)
    forced = (jrow == 0) | (jrow == cur) | (jrow == cur - 1)
    valid = (jrow * SLC_BLOCK) <= tj
    score = jnp.where(forced, 1e9, jnp.where(valid, imp, -1e9))
    sel = jnp.zeros((J, TQ), F32)
    for _ in range(SLC_TOP):
        mx = jnp.max(score, axis=0, keepdims=True)
        idx = jnp.min(jnp.where(score == mx, jrow, 1 << 30), axis=0, keepdims=True)
        hit = jrow == idx
        sel = jnp.where(hit & (mx > -1e8), 1.0, sel)
        score = jnp.where(hit, -3e38, score)
    selb = sel.astype(BF16)

    qr = stack_heads(qr_ref)
    acc_ref[...] = jnp.zeros(acc_ref.shape, F32)
    krow = lax.broadcasted_iota(jnp.int32, (TK, TQ), 0)
    tk_col = t0 + lax.broadcasted_iota(jnp.int32, (TK, TQ), 1)

    def sweep(kt, carry):
        m, l = carry
        k0 = pl.multiple_of(kt * TK, TK)
        st = lax.dot_general(ks_ref[pl.ds(k0, TK), :], qr, _NT, preferred_element_type=F32)
        bm = jnp.dot(ext_ref[kt], selb, preferred_element_type=F32)
        mask = (bm > 0.5) & ((k0 + krow) <= tk_col)
        ms, ls, als, pts = [], [], [], []
        for hh in range(N_HPG):
            sh = jnp.where(mask, st[:, cols[hh]], NEG)
            m_prev = m[:, cols[hh]]
            m_new = jnp.maximum(m_prev, jnp.max(sh, axis=0, keepdims=True))
            p = jnp.exp(sh - m_new)
            al = jnp.exp(m_prev - m_new)
            ms.append(m_new)
            als.append(al)
            ls.append(al * l[:, cols[hh]] + jnp.sum(p, axis=0, keepdims=True))
            pts.append(p.astype(BF16))
        acc_ref[...] = (acc_ref[...] * jnp.concatenate(als, axis=1)
                        + jnp.dot(vst_ref[kt], jnp.concatenate(pts, axis=1),
                                  preferred_element_type=F32))
        return jnp.concatenate(ms, axis=1), jnp.concatenate(ls, axis=1)

    m0 = jnp.full((1, NQ), NEG, F32)
    l0 = jnp.zeros((1, NQ), F32)
    _, l_s = lax.fori_loop(0, (t0 + TQ + TK - 1) // TK, sweep, (m0, l0))
    o_slc = acc_ref[...] * (1.0 / jnp.where(l_s > 0.0, l_s, 1.0))

    span = WINDOW + TQ
    wt0 = jnp.maximum(i - WINDOW // TQ, 0)
    ws = pl.multiple_of(wt0 * TQ, TQ)
    sw = lax.dot_general(kw_ref[pl.ds(ws, span), :], qr, _NT, preferred_element_type=F32)
    kp = ws + lax.broadcasted_iota(jnp.int32, (span, TQ), 0)
    tq_w = t0 + lax.broadcasted_iota(jnp.int32, (span, TQ), 1)
    wmask = (kp <= tq_w) & (kp > tq_w - WINDOW)
    pws, lws = [], []
    for hh in range(N_HPG):
        sh = jnp.where(wmask, sw[:, cols[hh]], NEG)
        p = jnp.exp(sh - jnp.max(sh, axis=0, keepdims=True))
        lws.append(jnp.sum(p, axis=0, keepdims=True))
        pws.append(p.astype(BF16))
    pw = jnp.concatenate(pws, axis=1)
    o_win = jnp.dot(vwt_ref[wt0], pw[0:TQ, :], preferred_element_type=F32)
    for j in range(1, span // TQ):
        o_win = o_win + jnp.dot(vwt_ref[wt0 + j], pw[j * TQ:(j + 1) * TQ, :],
                                preferred_element_type=F32)
    o_win = o_win * (1.0 / jnp.concatenate(lws, axis=1))

    gt = jax.nn.sigmoid(gate_ref[...]).T
    for hh in range(N_HPG):
        o = (gt[3 * hh:3 * hh + 1, :] * o_cmp[:, cols[hh]]
             + gt[3 * hh + 1:3 * hh + 2, :] * o_slc[:, cols[hh]]
             + gt[3 * hh + 2:3 * hh + 3, :] * o_win[:, cols[hh]])
        o_ref[:, hh * d:(hh + 1) * d] = o.T.astype(o_ref.dtype)


def _nsa_attn(q_s, q_r, kcmp, vcmp_t, ks_r, vs_t, kw_r, vw_t, gates, overlap_t, expand_t):
    B, S, _ = q_s.shape
    G, d, TQ = N_KV_GROUPS, N_HEAD_DIM, Q_BLOCK
    NC = kcmp.shape[1]
    TK = vs_t.shape[-1]
    gw = N_HPG * d
    qspec = pl.BlockSpec((None, TQ, gw), lambda b, g, i: (b, i, g))
    kspec = pl.BlockSpec((None, S, d), lambda b, g, i: (b, 0, g))
    tspec = lambda a: pl.BlockSpec((None, None) + a.shape[2:],
                                   lambda b, g, i: (b, g) + (0,) * (a.ndim - 2))
    const = lambda a: pl.BlockSpec(a.shape, lambda b, g, i: (0,) * a.ndim)
    return pl.pallas_call(
        functools.partial(_nsa_attn_kernel, TQ=TQ, TK=TK),
        grid=(B, G, S // TQ),
        in_specs=[qspec, qspec,
                  pl.BlockSpec((None, NC, d), lambda b, g, i: (b, 0, g)), tspec(vcmp_t),
                  kspec, tspec(vs_t), kspec, tspec(vw_t),
                  pl.BlockSpec((None, TQ, LANES), lambda b, g, i: (b, i, g)),
                  const(overlap_t), const(expand_t)],
        out_specs=qspec,
        out_shape=jax.ShapeDtypeStruct((B, S, G * gw), BF16),
        scratch_shapes=[pltpu.VMEM((d, N_HPG * TQ), F32)],
        compiler_params=_cparams(3),
        name="nsa_attn",
    )(q_s, q_r, kcmp, vcmp_t, ks_r, vs_t, kw_r, vw_t, gates, overlap_t, expand_t)


def _mix_kernel(hm_ref, on_ref, ga_ref, gb_ref, x_ref, wm_ref, wn_ref, wo_ref, g2_ref,
                h1_ref, xn2_ref):
    ya = jnp.dot(hm_ref[...], wm_ref[...], preferred_element_type=F32)
    yb = jnp.dot(on_ref[...], wn_ref[...], preferred_element_type=F32)
    mix = jax.nn.sigmoid(ga_ref[...]) * ya + jax.nn.sigmoid(gb_ref[...]) * yb
    h1 = x_ref[...] + jnp.dot(mix.astype(BF16), wo_ref[...], preferred_element_type=F32)
    h1_ref[...] = h1
    xn2_ref[...] = _rms(h1, g2_ref[...]).astype(BF16)


def _mix(hm, on, g_a, g_b, x2d, w_m, w_n, w_o, g2, tm):
    T, D = x2d.shape
    tok = pl.BlockSpec((tm, D), lambda i: (i, 0))
    wsp = pl.BlockSpec((D, D), lambda i: (0, 0))
    return pl.pallas_call(
        _mix_kernel,
        grid=(T // tm,),
        in_specs=[tok, tok, tok, tok, tok, wsp, wsp, wsp, pl.BlockSpec((1, D), lambda i: (0, 0))],
        out_specs=[tok, tok],
        out_shape=[jax.ShapeDtypeStruct((T, D), F32), jax.ShapeDtypeStruct((T, D), BF16)],
        compiler_params=_cparams(1),
        name="mix",
    )(hm, on, g_a, g_b, x2d, w_m, w_n, w_o, g2.reshape(1, D))


SUBLANES = 8


def _batcher_pairs(n):
    pairs = []

    def merge(lo, hi, r):
        step = 2 * r
        if step < hi - lo:
            merge(lo, hi, step)
            merge(lo + r, hi, step)
            pairs.extend((i, i + r) for i in range(lo + r, hi - r, step))
        else:
            pairs.append((lo, lo + r))

    def sort(lo, hi):
        if hi - lo >= 1:
            mid = lo + (hi - lo) // 2
            sort(lo, mid)
            sort(mid + 1, hi)
            merge(lo, hi, 1)

    sort(0, n - 1)
    return pairs


def _compare_exchange(xs, i, j):
    xs[i], xs[j] = jnp.maximum(xs[i], xs[j]), jnp.minimum(xs[i], xs[j])


def _sorted_top(xs, n_real=None):
    k = len(xs)
    n_real = k if n_real is None else n_real
    xs = list(xs)
    for i, j in _batcher_pairs(k):
        if j < n_real:
            _compare_exchange(xs, i, j)
    shift = SUBLANES // 2
    while shift:
        ys = [pltpu.roll(x, shift, axis=0) for x in xs]
        xs = [jnp.maximum(xs[i], ys[k - 1 - i]) for i in range(k)]
        d = k // 2
        while d:
            for i in range(k):
                if not i & d:
                    _compare_exchange(xs, i, i + d)
            d //= 2
        shift //= 2
    return xs


def _prefix_count(test, vals):
    c8 = test(vals[7])
    c4 = test(jnp.where(c8, vals[11], vals[3]))
    c2 = test(jnp.where(c8, jnp.where(c4, vals[13], vals[9]), jnp.where(c4, vals[5], vals[1])))
    hi = jnp.where(c4, jnp.where(c2, vals[14], vals[12]), jnp.where(c2, vals[10], vals[8]))
    lo = jnp.where(c4, jnp.where(c2, vals[6], vals[4]), jnp.where(c2, vals[2], vals[0]))
    c1 = test(jnp.where(c8, hi, lo))
    c16 = test(vals[15])
    f = lambda c, v: jnp.where(c, v, 0.0)
    return f(c8, 8.0) + f(c4, 4.0) + f(c2, 2.0) + f(c1, 1.0) + f(c16, 1.0)


def _peer_score_kernel(xn_ref, wq_ref, k1_ref, k2_ref, rc_ref, cf_ref, r2_ref, e2_ref):
    q = jnp.dot(xn_ref[...], wq_ref[...], preferred_element_type=F32)
    nk = P_KEYS
    n_steps = rc_ref.shape[0]
    per = nk // n_steps
    for h in range(P_HEADS):
        qh = q[:, h * 2 * P_HALF:(h + 1) * 2 * P_HALF].astype(BF16)
        s1 = lax.dot_general(k1_ref[...], qh, _NT, preferred_element_type=F32)
        s2 = lax.dot_general(k2_ref[...], qh, _NT, preferred_element_type=F32)
        groups = lambda s: [s[SUBLANES * g:SUBLANES * (g + 1), :] for g in range(nk // SUBLANES)]
        v1 = _sorted_top(groups(s1))
        v2 = _sorted_top(groups(s2))
        sub = lax.broadcasted_iota(jnp.int32, v1[0].shape, 0)
        lay = lambda vs: functools.reduce(lambda acc, r: jnp.where(sub == r, vs[r], acc),
                                          range(1, SUBLANES), vs[0])
        v2lo, v2hi, v1hi = lay(v2[:SUBLANES]), lay(v2[SUBLANES:]), lay(v1[SUBLANES:])
        cands = [v1[0] + v2lo, v1[0] + v2hi, v1[1] + v2lo]
        cands += [jnp.where(sub < P_TOPK // (a + 1), v1[a] + v2lo, -jnp.inf)
                  for a in range(2, SUBLANES)]
        cands.append(v1hi + v2[0])
        n_real = len(cands)
        cands += [jnp.full(sub.shape, -jnp.inf, F32)] * (P_TOPK - n_real)
        top = [t[0:1, :] for t in _sorted_top(cands, n_real)]
        v1r = [v[0:1, :] for v in v1]
        v2r = [v[0:1, :] for v in v2]
        z = jnp.exp(top[0] - top[0])
        for t in top[1:]:
            z = z + jnp.exp(t - top[0])
        rank2 = _prefix_count(lambda p: p > s2, v2r)
        count1 = _prefix_count(lambda p: s1 + p >= top[-1], v2r) - 1.0
        coef = jnp.exp(s1 - v1r[0]) / z
        for k in range(n_steps):
            rc_ref[k, h * per:(h + 1) * per, :] = count1[k * per:(k + 1) * per, :]
            cf_ref[k, h * per:(h + 1) * per, :] = coef[k * per:(k + 1) * per, :]
        rows = slice(h * nk, (h + 1) * nk)
        r2_ref[rows, :] = rank2
        e2_ref[rows, :] = jnp.exp(s2 - v2r[0])


def _peer_score(xn2, wq, k1p, k2p, tt, n_steps):
    T, D = xn2.shape
    nt = T // tt
    R = P_HEADS * P_KEYS
    fsp = pl.BlockSpec((None, n_steps, R // n_steps, tt), lambda i: (i, 0, 0, 0))
    fsh = jax.ShapeDtypeStruct((nt, n_steps, R // n_steps, tt), F32)
    ssp = pl.BlockSpec((None, R, tt), lambda i: (i, 0, 0))
    ssh = jax.ShapeDtypeStruct((nt, R, tt), F32)
    return pl.pallas_call(
        _peer_score_kernel,
        grid=(nt,),
        in_specs=[pl.BlockSpec((tt, D), lambda i: (i, 0)),
                  pl.BlockSpec(wq.shape, lambda i: (0, 0)),
                  pl.BlockSpec(k1p.shape, lambda i: (0, 0)),
                  pl.BlockSpec(k2p.shape, lambda i: (0, 0))],
        out_specs=[fsp, fsp, ssp, ssp],
        out_shape=[fsh, fsh, ssh, ssh],
        compiler_params=_cparams(1),
        name="peer_score",
    )(xn2, wq, k1p, k2p)


PEER_ROWS = 16
PEER_COLS = 256
PEER_STEPS = 8
PEER_CHUNKS = 2


def _routing_chunk(rc_ref, cf_ref, r2_ref, e2_ref, act_ref, p_ref, il0, n_il):
    nk = P_KEYS
    tt = act_ref.shape[1]
    per = rc_ref.shape[0] // P_HEADS
    blk = (PEER_ROWS, PEER_COLS)
    for il in range(il0, il0 + n_il):
        for lb in range(tt // PEER_COLS):
            lanes = slice(lb * PEER_COLS, (lb + 1) * PEER_COLS)
            rc = [jnp.broadcast_to(rc_ref[h * per + il:h * per + il + 1, lanes], blk)
                  for h in range(P_HEADS)]
            cf = [jnp.broadcast_to(cf_ref[h * per + il:h * per + il + 1, lanes], blk)
                  for h in range(P_HEADS)]
            for rb in range(nk // PEER_ROWS):
                w = None
                for h in range(P_HEADS):
                    rows = slice(h * nk + rb * PEER_ROWS, h * nk + (rb + 1) * PEER_ROWS)
                    t = jnp.where(r2_ref[rows, lanes] <= rc[h], cf[h] * e2_ref[rows, lanes], 0.0)
                    w = t if w is None else w + t
                dst = slice((il - il0) * nk + rb * PEER_ROWS, (il - il0) * nk + (rb + 1) * PEER_ROWS)
                p_ref[dst, lanes] = (w * _gelu(act_ref[dst, lanes])).astype(BF16)


def _peer_dense_kernel(xn_ref, rc_ref, cf_ref, r2_ref, e2_ref, u_ref, vt_ref, h1_ref, gf_ref,
                       out_ref, *scratch):
    acts = scratch[:PEER_CHUNKS]
    ps = scratch[PEER_CHUNKS:2 * PEER_CHUNKS]
    acc_ref = scratch[2 * PEER_CHUNKS]
    ci = pl.program_id(1)
    ec = u_ref.shape[0] // PEER_CHUNKS
    n_il = ec // P_KEYS

    @pl.when(ci == 0)
    def _init():
        acc_ref[...] = jnp.zeros_like(acc_ref)

    xn = xn_ref[...]
    for j in range(PEER_CHUNKS):
        acts[j][...] = lax.dot_general(u_ref[j * ec:(j + 1) * ec, :], xn, _NT,
                                       preferred_element_type=F32)
    for j in range(PEER_CHUNKS):
        _routing_chunk(rc_ref, cf_ref, r2_ref, e2_ref, acts[j], ps[j], j * n_il, n_il)
        acc_ref[...] += jnp.dot(vt_ref[:, j * ec:(j + 1) * ec], ps[j][...],
                                preferred_element_type=F32)

    @pl.when(ci == pl.num_programs(1) - 1)
    def _finish():
        y = h1_ref[...] + acc_ref[...].T
        out_ref[...] = _rms(y, gf_ref[...])


def _peer_dense(xn2, rc, cf, r2, e2, u, vt, h1, gf, tt):
    T, D = xn2.shape
    E = u.shape[0]
    R = P_HEADS * P_KEYS
    n_steps = rc.shape[1]
    step = E // n_steps
    ec = step // PEER_CHUNKS
    fsp = pl.BlockSpec((None, None, R // n_steps, tt), lambda t, k: (t, k, 0, 0))
    ssp = pl.BlockSpec((None, R, tt), lambda t, k: (t, 0, 0))
    tok = pl.BlockSpec((tt, D), lambda t, k: (t, 0))
    return pl.pallas_call(
        _peer_dense_kernel,
        grid=(T // tt, n_steps),
        in_specs=[tok, fsp, fsp, ssp, ssp,
                  pl.BlockSpec((step, D), lambda t, k: (k, 0)),
                  pl.BlockSpec((D, step), lambda t, k: (0, k)),
                  pl.BlockSpec((tt, D), lambda t, k: (t, 0), pipeline_mode=pl.Buffered(1)),
                  pl.BlockSpec((1, D), lambda t, k: (0, 0))],
        out_specs=tok,
        out_shape=jax.ShapeDtypeStruct((T, D), F32),
        scratch_shapes=([pltpu.VMEM((ec, tt), F32)] * PEER_CHUNKS
                        + [pltpu.VMEM((ec, tt), BF16)] * PEER_CHUNKS
                        + [pltpu.VMEM((D, tt), F32)]),
        compiler_params=_cparams(2),
        name="peer_dense",
    )(xn2, rc, cf, r2, e2, u, vt, h1, gf.reshape(1, D))


def _pad_cols(w, width):
    return jnp.pad(w, ((0, 0), (0, width - w.shape[1])))


def _segment_weights(w1):
    d, G = N_HEAD_DIM, N_KV_GROUPS
    half = CMP_BLOCK // 2
    w = w1.reshape(2, half, d, d)
    eye = jnp.eye(G, dtype=w1.dtype)
    w = w[:, :, None, :, None, :] * eye[None, None, :, None, :, None]
    return w.reshape(2, half * G * d, G * d)


def _segment_pos(pos):
    d, G = N_HEAD_DIM, N_KV_GROUPS
    half = CMP_BLOCK // 2
    p = jnp.broadcast_to(pos.reshape(2, half, 1, d), (2, half, G, d))
    return p.reshape(2, half * G * d)


def _values_t(v, tile):
    B, S, _ = v.shape
    v = v.reshape(B, S // tile, tile, N_KV_GROUPS, N_HEAD_DIM)
    return v.transpose(0, 3, 1, 4, 2)


def _forward(x, ln_mix_g, w_in, m_conv_w, m_conv_b, m_i_bias, m_f_bias, m_norm_g, w_m_out, cmp_k_pos, cmp_k_w1, cmp_k_w2, cmp_v_pos, cmp_v_w1, cmp_v_w2, w_n_out, w_out, ln_ffn_g, peer_wq, peer_k1, peer_k2, peer_u, peer_v, ln_f_g):
    B, S, D = x.shape
    T = B * S
    assert ln_mix_g.shape[0] == 1, "the final norm is fused into the last stage of a single layer"
    assert S % Q_BLOCK == 0 and S >= WINDOW + Q_BLOCK and S // SLC_BLOCK >= SLC_TOP
    l = 0
    d, G, H = N_HEAD_DIM, N_KV_GROUPS, M_HEADS
    MW = D
    KVW = G * d
    h = x.reshape(T, D)

    inv = ROPE_THETA ** (-jnp.arange(0, d, 2, dtype=F32) / d)
    ang = jnp.arange(S, dtype=F32)[:, None] * inv[None, :]
    ang = jnp.concatenate([ang, ang], axis=-1)
    cos = jnp.cos(ang)
    sin_signed = jnp.sin(ang) * jnp.concatenate([-jnp.ones((d // 2,), F32), jnp.ones((d // 2,), F32)])
    ncp = S // CMP_STRIDE
    n_slc = S // SLC_BLOCK
    n_idx = jnp.arange(ncp)[None, :] * CMP_STRIDE
    j_idx = jnp.arange(n_slc)[:, None] * SLC_BLOCK
    overlap_t = ((n_idx < j_idx + SLC_BLOCK) & (n_idx + CMP_BLOCK > j_idx)).astype(F32)
    tk = min(1024, S)
    expand_t = (jnp.arange(S).reshape(S // tk, tk, 1) // SLC_BLOCK
                == jnp.arange(n_slc)[None, None, :]).astype(BF16)

    wl = w_in[l]
    offs = [0]
    for wdt in (2 * MW, MW, MW, H, H, D, KVW, KVW, KVW, KVW, KVW, KVW, 3 * N_Q_HEADS, D, D):
        offs.append(offs[-1] + wdt)
    col = lambda i: wl[:, offs[i]:offs[i + 1]]
    w_small = _pad_cols(jnp.concatenate([col(3), col(4)], axis=1), LANES)
    w_ng = col(12).reshape(D, G, 3 * N_HPG)
    w_ng = jnp.pad(w_ng, ((0, 0), (0, 0), (0, LANES - 3 * N_HPG))).reshape(D, G * LANES)
    groups = [(col(0), F32), (col(1), BF16), (col(2), F32), (col(5), F32),
              (col(6), F32), (col(7), F32), (col(8), F32), (col(9), BF16),
              (col(10), F32), (col(11), BF16), (col(13), F32), (col(14), F32),
              (w_small, F32), (w_ng, F32)]
    w_cat = jnp.concatenate([g[0] for g in groups], axis=1).astype(BF16)
    (m_qk, m_v, m_o, n_q, n_kc, n_vc, n_ks, n_vs, n_kw, n_vw, g_a, g_b, small, n_g) = _in_proj(
        h, ln_mix_g[l], w_cat, [g[0].shape[1] for g in groups], [g[1] for g in groups], tm=256)

    small3 = small.reshape(B, S, LANES)
    grow = small3[:, :, :8].reshape(B, S // M_CHUNK, M_CHUNK, 8).transpose(0, 1, 3, 2)
    bias = jnp.concatenate([m_i_bias[l], m_f_bias[l]])
    hm = _mlstm(m_qk.reshape(B, S, 2 * MW), m_v.reshape(B, S, MW), m_o.reshape(B, S, MW),
                small3, grow, _pad_cols(bias.reshape(1, 2 * H), LANES), bias.reshape(2 * H, 1),
                m_conv_w[l], m_conv_b[l].reshape(1, 2 * MW), m_norm_g[l].reshape(1, MW))

    q_s, q_r, ks_r, kw_r = _nsa_prep(n_q.reshape(B, S, D), n_ks.reshape(B, S, KVW),
                                     n_kw.reshape(B, S, KVW), cos, sin_signed, ts=min(512, S))
    seg = CMP_STRIDE * KVW
    kcmp = _compress(n_kc.reshape(B, ncp, seg), _segment_pos(cmp_k_pos[l]),
                     _segment_weights(cmp_k_w1[l]).astype(BF16), cmp_k_w2[l].astype(BF16))
    vcmp = _compress(n_vc.reshape(B, ncp, seg), _segment_pos(cmp_v_pos[l]),
                     _segment_weights(cmp_v_w1[l]).astype(BF16), cmp_v_w2[l].astype(BF16))
    vcmp_t = vcmp.reshape(B, ncp, G, d).transpose(0, 2, 3, 1)
    on = _nsa_attn(q_s, q_r, kcmp, vcmp_t, ks_r, _values_t(n_vs.reshape(B, S, KVW), tk), kw_r,
                   _values_t(n_vw.reshape(B, S, KVW), Q_BLOCK), n_g.reshape(B, S, G * LANES),
                   overlap_t, expand_t)

    h1, xn2 = _mix(hm.reshape(T, MW), on.reshape(T, D), g_a, g_b, h,
                   w_m_out[l].astype(BF16), w_n_out[l].astype(BF16), w_out[l].astype(BF16),
                   ln_ffn_g[l], tm=256)

    tt = min(512, T)
    k1p = jnp.pad(peer_k1[l], ((0, 0), (0, P_HALF))).astype(BF16)
    k2p = jnp.pad(peer_k2[l], ((0, 0), (P_HALF, 0))).astype(BF16)
    rc, cf, r2, e2 = _peer_score(xn2, peer_wq[l].astype(BF16), k1p, k2p, tt, PEER_STEPS)
    out = _peer_dense(xn2, rc, cf, r2, e2, peer_u[l].astype(BF16), peer_v[l].T.astype(BF16),
                      h1, ln_f_g, tt)
    aux = dict(m_qk=m_qk, m_v=m_v, small=small, hm=hm, q_s=q_s, q_r=q_r, ks_r=ks_r, kw_r=kw_r,
               kcmp=kcmp, vcmp=vcmp, on=on, h1=h1, xn2=xn2, rc=rc, cf=cf, r2=r2, e2=e2)
    return out.reshape(B, S, D), aux


def kernel(x, ln_mix_g, w_in, m_conv_w, m_conv_b, m_i_bias, m_f_bias, m_norm_g, w_m_out, cmp_k_pos, cmp_k_w1, cmp_k_w2, cmp_v_pos, cmp_v_w1, cmp_v_w2, w_n_out, w_out, ln_ffn_g, peer_wq, peer_k1, peer_k2, peer_u, peer_v, ln_f_g):
    return _forward(x, ln_mix_g, w_in, m_conv_w, m_conv_b, m_i_bias, m_f_bias, m_norm_g, w_m_out,
                    cmp_k_pos, cmp_k_w1, cmp_k_w2, cmp_v_pos, cmp_v_w1, cmp_v_w2, w_n_out, w_out,
                    ln_ffn_g, peer_wq, peer_k1, peer_k2, peer_u, peer_v, ln_f_g)[0]
```

```python
import functools

import jax
import jax.numpy as jnp
from jax import lax
from jax.experimental import pallas as pl
from jax.experimental.pallas import tpu as pltpu

F32 = jnp.float32
BF16 = jnp.bfloat16

EPS = 1e-6
M_HEADS = 4
M_CONV = 4
M_CHUNK = 64
N_Q_HEADS = 8
N_KV_GROUPS = 2
N_HPG = N_Q_HEADS // N_KV_GROUPS
N_HEAD_DIM = 128
CMP_BLOCK = 32
CMP_STRIDE = 16
SLC_BLOCK = 64
SLC_TOP = 8
WINDOW = 512
Q_BLOCK = 128
ROPE_THETA = 10000.0
P_HEADS = 8
P_KEYS = 128
P_HALF = 64
P_TOPK = 16

LANES = 128
VMEM_LIMIT = 56 * 1024 * 1024
NEG = -1e30

_NT = (((1,), (1,)), ((), ()))
_TN = (((0,), (0,)), ((), ()))


def _cparams(n_axes, flags=None):
    return pltpu.CompilerParams(dimension_semantics=("arbitrary",) * n_axes,
                                vmem_limit_bytes=VMEM_LIMIT, flags=flags)


def _rms(x, g):
    return x * lax.rsqrt(jnp.mean(x * x, axis=-1, keepdims=True) + EPS) * g


def _gelu(x):
    return 0.5 * x * (1.0 + lax.erf(x * (2.0 ** -0.5)))


def _in_proj_kernel(x_ref, g_ref, w_ref, *o_refs, widths):
    xn = _rms(x_ref[...], g_ref[...]).astype(BF16)
    off = 0
    for o_ref, w in zip(o_refs, widths):
        o_ref[...] = jnp.dot(xn, w_ref[:, off:off + w],
                             preferred_element_type=F32).astype(o_ref.dtype)
        off += w


def _in_proj(x2d, g, w, widths, dtypes, tm):
    T, D = x2d.shape
    N = w.shape[1]
    return pl.pallas_call(
        functools.partial(_in_proj_kernel, widths=tuple(widths)),
        grid=(T // tm,),
        in_specs=[pl.BlockSpec((tm, D), lambda i: (i, 0)),
                  pl.BlockSpec((1, D), lambda i: (0, 0)),
                  pl.BlockSpec((D, N), lambda i: (0, 0), pipeline_mode=pl.Buffered(1))],
        out_specs=[pl.BlockSpec((tm, wd), lambda i: (i, 0)) for wd in widths],
        out_shape=[jax.ShapeDtypeStruct((T, wd), dt) for wd, dt in zip(widths, dtypes)],
        compiler_params=_cparams(1),
        name="in_proj",
    )(x2d, g.reshape(1, D), w)


def _log_sigmoid(x):
    return jnp.minimum(x, 0.0) - jnp.log1p(jnp.exp(-jnp.abs(x)))


def _mlstm_kernel(qk_ref, v_ref, og_ref, gcol_ref, grow_ref, brow_ref, bcol_ref, cw_ref, cb_ref,
                  ng_ref, out_ref, ext_ref, ct_ref, n_ref, m_ref, *, L, H, dh):
    W = H * dh

    @pl.when(pl.program_id(1) == 0)
    def _init():
        ext_ref[0:8, :] = jnp.zeros((8, 2 * W), F32)
        ct_ref[...] = jnp.zeros_like(ct_ref)
        n_ref[...] = jnp.zeros_like(n_ref)
        m_ref[...] = jnp.zeros_like(m_ref)

    x = qk_ref[...]
    ext_ref[8:8 + L, :] = x
    pre = cb_ref[...] + cw_ref[0:1, :] * ext_ref[8 - (M_CONV - 1):8 - (M_CONV - 1) + L, :]
    for j in range(1, M_CONV):
        o = 8 - (M_CONV - 1) + j
        pre = pre + cw_ref[j:j + 1, :] * ext_ref[o:o + L, :]
    ext_ref[0:8, :] = x[L - 8:L, :]
    qk = pre * jax.nn.sigmoid(pre)

    gcol = gcol_ref[...] + brow_ref[...]
    grow = grow_ref[...] + bcol_ref[...]
    rr = lax.broadcasted_iota(jnp.int32, (L, L), 0)
    cc = lax.broadcasted_iota(jnp.int32, (L, L), 1)
    causal = cc <= rr
    lower = causal.astype(F32)

    for h in range(H):
        q = qk[:, h * dh:(h + 1) * dh]
        k = qk[:, W + h * dh:W + (h + 1) * dh] * (dh ** -0.5)
        v = v_ref[:, h * dh:(h + 1) * dh].astype(F32)
        li_c = gcol[:, h:h + 1]
        lf_c = _log_sigmoid(gcol[:, H + h:H + h + 1])
        li_r = grow[h:h + 1, :]
        lf_r = _log_sigmoid(grow[H + h:H + h + 1, :])
        b_c = jnp.sum(lower * lf_r, axis=1, keepdims=True)
        b_r = jnp.sum((rr <= cc).astype(F32) * lf_c, axis=0, keepdims=True)
        m_prev = m_ref[h:h + 1, 0:1]

        dm = jnp.where(causal, b_c - b_r + li_r, -jnp.inf)
        inter = b_c + m_prev
        mt = jnp.maximum(inter, jnp.max(dm, axis=1, keepdims=True))
        dw = jnp.exp(dm - mt)
        iw = jnp.exp(inter - mt)
        qb = q.astype(BF16)
        kb = k.astype(BF16)
        wm = lax.dot_general(qb, kb, _NT, preferred_element_type=F32) * dw
        ct = ct_ref[h]
        n_row = n_ref[h:h + 1, :]
        num = (iw * jnp.dot(qb, ct.astype(BF16), preferred_element_type=F32)
               + jnp.dot(wm.astype(BF16), v.astype(BF16), preferred_element_type=F32))
        den = iw * jnp.sum(q * n_row, axis=1, keepdims=True) + jnp.sum(wm, axis=1, keepdims=True)
        hh = num / jnp.maximum(jnp.abs(den), jnp.exp(-mt))

        b_last = b_c[L - 1:L, :]
        g_r = b_last - b_r + li_r
        g_c = b_last - b_c + li_c
        m_new = jnp.maximum(b_last + m_prev, jnp.max(g_r, axis=1, keepdims=True))
        a = jnp.exp(b_last + m_prev - m_new)
        w_c = jnp.exp(g_c - m_new)
        ct_ref[h] = a * ct + lax.dot_general(kb, (v * w_c).astype(BF16), _TN,
                                             preferred_element_type=F32)
        n_ref[h:h + 1, :] = a * n_row + jnp.sum(k * w_c, axis=0, keepdims=True)
        m_ref[h:h + 1, :] = jnp.broadcast_to(m_new, (1, LANES))

        hn = hh * lax.rsqrt(jnp.mean(hh * hh, axis=1, keepdims=True) + EPS)
        o = hn * ng_ref[:, h * dh:(h + 1) * dh] * jax.nn.sigmoid(og_ref[:, h * dh:(h + 1) * dh])
        out_ref[:, h * dh:(h + 1) * dh] = o.astype(out_ref.dtype)


def _mlstm(m_qk, m_v, m_o, small, grow, brow, bcol, conv_w, conv_b, norm_g):
    B, S, W2 = m_qk.shape
    W = W2 // 2
    H, L = M_HEADS, M_CHUNK
    dh = W // H
    nc = S // L
    return pl.pallas_call(
        functools.partial(_mlstm_kernel, L=L, H=H, dh=dh),
        grid=(B, nc),
        in_specs=[pl.BlockSpec((None, L, W2), lambda b, c: (b, c, 0)),
                  pl.BlockSpec((None, L, W), lambda b, c: (b, c, 0)),
                  pl.BlockSpec((None, L, W), lambda b, c: (b, c, 0)),
                  pl.BlockSpec((None, L, LANES), lambda b, c: (b, c, 0)),
                  pl.BlockSpec((None, None, 8, L), lambda b, c: (b, c, 0, 0)),
                  pl.BlockSpec((1, LANES), lambda b, c: (0, 0)),
                  pl.BlockSpec((8, 1), lambda b, c: (0, 0)),
                  pl.BlockSpec((M_CONV, W2), lambda b, c: (0, 0)),
                  pl.BlockSpec((1, W2), lambda b, c: (0, 0)),
                  pl.BlockSpec((1, W), lambda b, c: (0, 0))],
        out_specs=pl.BlockSpec((None, L, W), lambda b, c: (b, c, 0)),
        out_shape=jax.ShapeDtypeStruct((B, S, W), BF16),
        scratch_shapes=[pltpu.VMEM((8 + L, W2), F32),
                        pltpu.VMEM((H, dh, dh), F32),
                        pltpu.VMEM((8, dh), F32),
                        pltpu.VMEM((8, LANES), F32)],
        compiler_params=_cparams(2),
        name="mlstm",
    )(m_qk, m_v, m_o, small, grow, brow, bcol, conv_w, conv_b, norm_g)


def _rope(x, cos, sin_signed):
    return x * cos + pltpu.roll(x, N_HEAD_DIM // 2, axis=1) * sin_signed


def _nsa_prep_kernel(q_ref, ks_ref, kw_ref, cos_ref, sin_ref, qs_out, qr_out, ks_out, kw_out):
    cos = cos_ref[...]
    sin = sin_ref[...]
    scale = N_HEAD_DIM ** -0.5
    d = N_HEAD_DIM
    for h in range(N_Q_HEADS):
        x = q_ref[:, h * d:(h + 1) * d] * scale
        qs_out[:, h * d:(h + 1) * d] = x.astype(BF16)
        qr_out[:, h * d:(h + 1) * d] = _rope(x, cos, sin).astype(BF16)
    for g in range(N_KV_GROUPS):
        ks_out[:, g * d:(g + 1) * d] = _rope(ks_ref[:, g * d:(g + 1) * d], cos, sin).astype(BF16)
        kw_out[:, g * d:(g + 1) * d] = _rope(kw_ref[:, g * d:(g + 1) * d], cos, sin).astype(BF16)


def _nsa_prep(n_q, n_ks, n_kw, cos, sin_signed, ts):
    B, S, NW = n_q.shape
    KW = n_ks.shape[-1]
    tok = lambda w: pl.BlockSpec((None, ts, w), lambda b, i: (b, i, 0))
    tab = pl.BlockSpec((ts, N_HEAD_DIM), lambda b, i: (i, 0))
    return pl.pallas_call(
        _nsa_prep_kernel,
        grid=(B, S // ts),
        in_specs=[tok(NW), tok(KW), tok(KW), tab, tab],
        out_specs=[tok(NW), tok(NW), tok(KW), tok(KW)],
        out_shape=[jax.ShapeDtypeStruct((B, S, NW), BF16), jax.ShapeDtypeStruct((B, S, NW), BF16),
                   jax.ShapeDtypeStruct((B, S, KW), BF16), jax.ShapeDtypeStruct((B, S, KW), BF16)],
        compiler_params=_cparams(2),
        name="nsa_prep",
    )(n_q, n_ks, n_kw, cos, sin_signed)


def _compress_kernel(x_ref, p_ref, wh_ref, w2_ref, o_ref):
    x = x_ref[...]
    ns = x.shape[0]
    y0 = jnp.dot((x + p_ref[0:1, :]).astype(BF16), wh_ref[0], preferred_element_type=F32)
    y1 = jnp.dot((x + p_ref[1:2, :]).astype(BF16), wh_ref[1], preferred_element_type=F32)
    act = _gelu(y0 + pltpu.roll(y1, ns - 1, axis=0))
    d = N_HEAD_DIM
    for g in range(N_KV_GROUPS):
        o_ref[:, g * d:(g + 1) * d] = jnp.dot(act[:, g * d:(g + 1) * d].astype(BF16), w2_ref[...],
                                              preferred_element_type=F32).astype(o_ref.dtype)


def _compress(x_seg, pos2, wh, w2):
    B, NS, SW = x_seg.shape
    GW = N_KV_GROUPS * N_HEAD_DIM
    return pl.pallas_call(
        _compress_kernel,
        grid=(B,),
        in_specs=[pl.BlockSpec((None, NS, SW), lambda b: (b, 0, 0)),
                  pl.BlockSpec((2, SW), lambda b: (0, 0)),
                  pl.BlockSpec((2, SW, GW), lambda b: (0, 0, 0)),
                  pl.BlockSpec((N_HEAD_DIM, N_HEAD_DIM), lambda b: (0, 0))],
        out_specs=pl.BlockSpec((None, NS, GW), lambda b: (b, 0, 0)),
        out_shape=jax.ShapeDtypeStruct((B, NS, GW), BF16),
        compiler_params=_cparams(1),
        name="nsa_compress",
    )(x_seg, pos2, wh, w2)


def _nsa_attn_kernel(qs_ref, qr_ref, kc_ref, vct_ref, ks_ref, vst_ref, kw_ref, vwt_ref, gate_ref,
                     ovt_ref, ext_ref, o_ref, acc_ref, *, TQ, TK):
    i = pl.program_id(2)
    t0 = i * TQ
    d = N_HEAD_DIM
    NC = kc_ref.shape[0]
    J = ovt_ref.shape[0]
    NQ = N_HPG * TQ
    cols = [slice(hh * TQ, (hh + 1) * TQ) for hh in range(N_HPG)]

    def stack_heads(ref):
        return jnp.concatenate([ref[:, hh * d:(hh + 1) * d] for hh in range(N_HPG)], axis=0)

    s = lax.dot_general(kc_ref[...], stack_heads(qs_ref), _NT, preferred_element_type=F32)
    nrow = lax.broadcasted_iota(jnp.int32, (NC, TQ), 0)
    tcol = t0 + lax.broadcasted_iota(jnp.int32, (NC, TQ), 1)
    cmask = (nrow * CMP_STRIDE + (CMP_BLOCK - 1)) <= tcol
    psum = jnp.zeros((NC, TQ), F32)
    ps = []
    for hh in range(N_HPG):
        sh = jnp.where(cmask, s[:, cols[hh]], NEG)
        p = jnp.where(cmask, jnp.exp(sh - jnp.max(sh, axis=0, keepdims=True)), 0.0)
        l = jnp.sum(p, axis=0, keepdims=True)
        p = p * (1.0 / jnp.where(l > 0.0, l, 1.0))
        psum = psum + p
        ps.append(p.astype(BF16))
    o_cmp = jnp.dot(vct_ref[...], jnp.concatenate(ps, axis=1), preferred_element_type=F32)

    imp = jnp.dot(ovt_ref[...], psum, preferred_element_type=F32, precision=lax.Precision.HIGHEST)
    jrow = lax.broadcasted_iota(jnp.int32, (J, TQ), 0)
    tj = t0 + lax.broadcasted_iota(jnp.int32, (J, TQ), 1)
    cur = jnp.right_shift(tj, SLC_BLOCK.bit_length() - 1)
    forced = (jrow == 0) | (jrow == cur) | (jrow == cur - 1)
    valid = (jrow * SLC_BLOCK) <= tj
    score = jnp.where(forced, 1e9, jnp.where(valid, imp, -1e9))
    sel = jnp.zeros((J, TQ), F32)
    for _ in range(SLC_TOP):
        mx = jnp.max(score, axis=0, keepdims=True)
        idx = jnp.min(jnp.where(score == mx, jrow, 1 << 30), axis=0, keepdims=True)
        hit = jrow == idx
        sel = jnp.where(hit & (mx > -1e8), 1.0, sel)
        score = jnp.where(hit, -3e38, score)
    selb = sel.astype(BF16)

    qr = stack_heads(qr_ref)
    acc_ref[...] = jnp.zeros(acc_ref.shape, F32)
    krow = lax.broadcasted_iota(jnp.int32, (TK, TQ), 0)
    tk_col = t0 + lax.broadcasted_iota(jnp.int32, (TK, TQ), 1)

    def sweep(kt, carry):
        m, l = carry
        k0 = pl.multiple_of(kt * TK, TK)
        st = lax.dot_general(ks_ref[pl.ds(k0, TK), :], qr, _NT, preferred_element_type=F32)
        bm = jnp.dot(ext_ref[kt], selb, preferred_element_type=F32)
        mask = (bm > 0.5) & ((k0 + krow) <= tk_col)
        ms, ls, als, pts = [], [], [], []
        for hh in range(N_HPG):
            sh = jnp.where(mask, st[:, cols[hh]], NEG)
            m_prev = m[:, cols[hh]]
            m_new = jnp.maximum(m_prev, jnp.max(sh, axis=0, keepdims=True))
            p = jnp.exp(sh - m_new)
            al = jnp.exp(m_prev - m_new)
            ms.append(m_new)
            als.append(al)
            ls.append(al * l[:, cols[hh]] + jnp.sum(p, axis=0, keepdims=True))
            pts.append(p.astype(BF16))
        acc_ref[...] = (acc_ref[...] * jnp.concatenate(als, axis=1)
                        + jnp.dot(vst_ref[kt], jnp.concatenate(pts, axis=1),
                                  preferred_element_type=F32))
        return jnp.concatenate(ms, axis=1), jnp.concatenate(ls, axis=1)

    m0 = jnp.full((1, NQ), NEG, F32)
    l0 = jnp.zeros((1, NQ), F32)
    _, l_s = lax.fori_loop(0, (t0 + TQ + TK - 1) // TK, sweep, (m0, l0))
    o_slc = acc_ref[...] * (1.0 / jnp.where(l_s > 0.0, l_s, 1.0))

    span = WINDOW + TQ
    wt0 = jnp.maximum(i - WINDOW // TQ, 0)
    ws = pl.multiple_of(wt0 * TQ, TQ)
    sw = lax.dot_general(kw_ref[pl.ds(ws, span), :], qr, _NT, preferred_element_type=F32)
    kp = ws + lax.broadcasted_iota(jnp.int32, (span, TQ), 0)
    tq_w = t0 + lax.broadcasted_iota(jnp.int32, (span, TQ), 1)
    wmask = (kp <= tq_w) & (kp > tq_w - WINDOW)
    pws, lws = [], []
    for hh in range(N_HPG):
        sh = jnp.where(wmask, sw[:, cols[hh]], NEG)
        p = jnp.exp(sh - jnp.max(sh, axis=0, keepdims=True))
        lws.append(jnp.sum(p, axis=0, keepdims=True))
        pws.append(p.astype(BF16))
    pw = jnp.concatenate(pws, axis=1)
    o_win = jnp.dot(vwt_ref[wt0], pw[0:TQ, :], preferred_element_type=F32)
    for j in range(1, span // TQ):
        o_win = o_win + jnp.dot(vwt_ref[wt0 + j], pw[j * TQ:(j + 1) * TQ, :],
                                preferred_element_type=F32)
    o_win = o_win * (1.0 / jnp.concatenate(lws, axis=1))

    gt = jax.nn.sigmoid(gate_ref[...]).T
    for hh in range(N_HPG):
        o = (gt[3 * hh:3 * hh + 1, :] * o_cmp[:, cols[hh]]
             + gt[3 * hh + 1:3 * hh + 2, :] * o_slc[:, cols[hh]]
             + gt[3 * hh + 2:3 * hh + 3, :] * o_win[:, cols[hh]])
        o_ref[:, hh * d:(hh + 1) * d] = o.T.astype(o_ref.dtype)


def _nsa_attn(q_s, q_r, kcmp, vcmp_t, ks_r, vs_t, kw_r, vw_t, gates, overlap_t, expand_t):
    B, S, _ = q_s.shape
    G, d, TQ = N_KV_GROUPS, N_HEAD_DIM, Q_BLOCK
    NC = kcmp.shape[1]
    TK = vs_t.shape[-1]
    gw = N_HPG * d
    qspec = pl.BlockSpec((None, TQ, gw), lambda b, g, i: (b, i, g))
    kspec = pl.BlockSpec((None, S, d), lambda b, g, i: (b, 0, g))
    tspec = lambda a: pl.BlockSpec((None, None) + a.shape[2:],
                                   lambda b, g, i: (b, g) + (0,) * (a.ndim - 2))
    const = lambda a: pl.BlockSpec(a.shape, lambda b, g, i: (0,) * a.ndim)
    return pl.pallas_call(
        functools.partial(_nsa_attn_kernel, TQ=TQ, TK=TK),
        grid=(B, G, S // TQ),
        in_specs=[qspec, qspec,
                  pl.BlockSpec((None, NC, d), lambda b, g, i: (b, 0, g)), tspec(vcmp_t),
                  kspec, tspec(vs_t), kspec, tspec(vw_t),
                  pl.BlockSpec((None, TQ, LANES), lambda b, g, i: (b, i, g)),
                  const(overlap_t), const(expand_t)],
        out_specs=qspec,
        out_shape=jax.ShapeDtypeStruct((B, S, G * gw), BF16),
        scratch_shapes=[pltpu.VMEM((d, N_HPG * TQ), F32)],
        compiler_params=_cparams(3),
        name="nsa_attn",
    )(q_s, q_r, kcmp, vcmp_t, ks_r, vs_t, kw_r, vw_t, gates, overlap_t, expand_t)


def _mix_kernel(hm_ref, on_ref, ga_ref, gb_ref, x_ref, wm_ref, wn_ref, wo_ref, g2_ref,
                h1_ref, xn2_ref):
    ya = jnp.dot(hm_ref[...], wm_ref[...], preferred_element_type=F32)
    yb = jnp.dot(on_ref[...], wn_ref[...], preferred_element_type=F32)
    mix = jax.nn.sigmoid(ga_ref[...]) * ya + jax.nn.sigmoid(gb_ref[...]) * yb
    h1 = x_ref[...] + jnp.dot(mix.astype(BF16), wo_ref[...], preferred_element_type=F32)
    h1_ref[...] = h1
    xn2_ref[...] = _rms(h1, g2_ref[...]).astype(BF16)


def _mix(hm, on, g_a, g_b, x2d, w_m, w_n, w_o, g2, tm):
    T, D = x2d.shape
    tok = pl.BlockSpec((tm, D), lambda i: (i, 0))
    wsp = pl.BlockSpec((D, D), lambda i: (0, 0))
    return pl.pallas_call(
        _mix_kernel,
        grid=(T // tm,),
        in_specs=[tok, tok, tok, tok, tok, wsp, wsp, wsp, pl.BlockSpec((1, D), lambda i: (0, 0))],
        out_specs=[tok, tok],
        out_shape=[jax.ShapeDtypeStruct((T, D), F32), jax.ShapeDtypeStruct((T, D), BF16)],
        compiler_params=_cparams(1),
        name="mix",
    )(hm, on, g_a, g_b, x2d, w_m, w_n, w_o, g2.reshape(1, D))


SUBLANES = 8


def _batcher_pairs(n):
    pairs = []

    def merge(lo, hi, r):
        step = 2 * r
        if step < hi - lo:
            merge(lo, hi, step)
            merge(lo + r, hi, step)
            pairs.extend((i, i + r) for i in range(lo + r, hi - r, step))
        else:
            pairs.append((lo, lo + r))

    def sort(lo, hi):
        if hi - lo >= 1:
            mid = lo + (hi - lo) // 2
            sort(lo, mid)
            sort(mid + 1, hi)
            merge(lo, hi, 1)

    sort(0, n - 1)
    return pairs


def _compare_exchange(xs, i, j):
    xs[i], xs[j] = jnp.maximum(xs[i], xs[j]), jnp.minimum(xs[i], xs[j])


def _sorted_top(xs, n_real=None):
    k = len(xs)
    n_real = k if n_real is None else n_real
    xs = list(xs)
    for i, j in _batcher_pairs(k):
        if j < n_real:
            _compare_exchange(xs, i, j)
    shift = SUBLANES // 2
    while shift:
        ys = [pltpu.roll(x, shift, axis=0) for x in xs]
        xs = [jnp.maximum(xs[i], ys[k - 1 - i]) for i in range(k)]
        d = k // 2
        while d:
            for i in range(k):
                if not i & d:
                    _compare_exchange(xs, i, i + d)
            d //= 2
        shift //= 2
    return xs


def _prefix_last(test, vals):
    p8 = vals[7]
    c8 = test(p8)
    p4 = jnp.where(c8, vals[11], vals[3])
    c4 = test(p4)
    p2 = jnp.where(c8, jnp.where(c4, vals[13], vals[9]), jnp.where(c4, vals[5], vals[1]))
    c2 = test(p2)
    hi = jnp.where(c4, jnp.where(c2, vals[14], vals[12]), jnp.where(c2, vals[10], vals[8]))
    lo = jnp.where(c4, jnp.where(c2, vals[6], vals[4]), jnp.where(c2, vals[2], vals[0]))
    p1 = jnp.where(c8, hi, lo)
    c1 = test(p1)
    best = jnp.where(c8, p8, jnp.inf)
    best = jnp.where(c4, p4, best)
    best = jnp.where(c2, p2, best)
    best = jnp.where(c1, p1, best)
    return jnp.where(test(vals[15]), vals[15], best)


def _peer_score_kernel(xn_ref, wq_ref, k1_ref, k2_ref, ec_ref, cf_ref, e2_ref):
    q = jnp.dot(xn_ref[...], wq_ref[...], preferred_element_type=F32)
    nk = P_KEYS
    n_steps = ec_ref.shape[0]
    per = nk // n_steps
    for h in range(P_HEADS):
        qh = q[:, h * 2 * P_HALF:(h + 1) * 2 * P_HALF].astype(BF16)
        s1 = lax.dot_general(k1_ref[...], qh, _NT, preferred_element_type=F32)
        s2 = lax.dot_general(k2_ref[...], qh, _NT, preferred_element_type=F32)
        groups = lambda s: [s[SUBLANES * g:SUBLANES * (g + 1), :] for g in range(nk // SUBLANES)]
        v1 = _sorted_top(groups(s1))
        v2 = _sorted_top(groups(s2))
        sub = lax.broadcasted_iota(jnp.int32, v1[0].shape, 0)
        lay = lambda vs: functools.reduce(lambda acc, r: jnp.where(sub == r, vs[r], acc),
                                          range(1, SUBLANES), vs[0])
        v2lo, v2hi, v1hi = lay(v2[:SUBLANES]), lay(v2[SUBLANES:]), lay(v1[SUBLANES:])
        cands = [v1[0] + v2lo, v1[0] + v2hi, v1[1] + v2lo]
        cands += [jnp.where(sub < P_TOPK // (a + 1), v1[a] + v2lo, -jnp.inf)
                  for a in range(2, SUBLANES)]
        cands.append(v1hi + v2[0])
        n_real = len(cands)
        cands += [jnp.full(sub.shape, -jnp.inf, F32)] * (P_TOPK - n_real)
        top = [t[0:1, :] for t in _sorted_top(cands, n_real)]
        v1r = [v[0:1, :] for v in v1]
        v2r = [v[0:1, :] for v in v2]
        z = jnp.exp(top[0] - top[0])
        for t in top[1:]:
            z = z + jnp.exp(t - top[0])
        cut = _prefix_last(lambda p: s1 + p >= top[-1], v2r)
        ecut = jnp.exp(cut - v2r[0])
        coef = jnp.exp(s1 - v1r[0]) / z
        for k in range(n_steps):
            ec_ref[k, h * per:(h + 1) * per, :] = ecut[k * per:(k + 1) * per, :]
            cf_ref[k, h * per:(h + 1) * per, :] = coef[k * per:(k + 1) * per, :]
        e2_ref[h * nk:(h + 1) * nk, :] = jnp.exp(s2 - v2r[0])


def _peer_score(xn2, wq, k1p, k2p, tt, n_steps):
    T, D = xn2.shape
    nt = T // tt
    R = P_HEADS * P_KEYS
    fsp = pl.BlockSpec((None, n_steps, R // n_steps, tt), lambda i: (i, 0, 0, 0))
    fsh = jax.ShapeDtypeStruct((nt, n_steps, R // n_steps, tt), F32)
    ssp = pl.BlockSpec((None, R, tt), lambda i: (i, 0, 0))
    ssh = jax.ShapeDtypeStruct((nt, R, tt), F32)
    return pl.pallas_call(
        _peer_score_kernel,
        grid=(nt,),
        in_specs=[pl.BlockSpec((tt, D), lambda i: (i, 0)),
                  pl.BlockSpec(wq.shape, lambda i: (0, 0)),
                  pl.BlockSpec(k1p.shape, lambda i: (0, 0)),
                  pl.BlockSpec(k2p.shape, lambda i: (0, 0))],
        out_specs=[fsp, fsp, ssp],
        out_shape=[fsh, fsh, ssh],
        compiler_params=_cparams(1),
        name="peer_score",
    )(xn2, wq, k1p, k2p)


PEER_ROWS = 16
PEER_COLS = 256
PEER_STEPS = 8
PEER_CHUNKS = 2


def _routing_chunk(ec_ref, cf_ref, e2_ref, act_ref, p_ref, il0, n_il):
    nk = P_KEYS
    tt = act_ref.shape[1]
    per = ec_ref.shape[0] // P_HEADS
    blk = (PEER_ROWS, PEER_COLS)
    for il in range(il0, il0 + n_il):
        for lb in range(tt // PEER_COLS):
            lanes = slice(lb * PEER_COLS, (lb + 1) * PEER_COLS)
            ec = [jnp.broadcast_to(ec_ref[h * per + il:h * per + il + 1, lanes], blk)
                  for h in range(P_HEADS)]
            cf = [jnp.broadcast_to(cf_ref[h * per + il:h * per + il + 1, lanes], blk)
                  for h in range(P_HEADS)]
            for rb in range(nk // PEER_ROWS):
                w = None
                for h in range(P_HEADS):
                    e2 = e2_ref[h * nk + rb * PEER_ROWS:h * nk + (rb + 1) * PEER_ROWS, lanes]
                    t = jnp.where(e2 >= ec[h], cf[h] * e2, 0.0)
                    w = t if w is None else w + t
                dst = slice((il - il0) * nk + rb * PEER_ROWS, (il - il0) * nk + (rb + 1) * PEER_ROWS)
                p_ref[dst, lanes] = (w * _gelu(act_ref[dst, lanes])).astype(BF16)


def _peer_dense_kernel(xn_ref, ec_ref, cf_ref, e2_ref, u_ref, vt_ref, h1_ref, gf_ref,
                       out_ref, *scratch):
    acts = scratch[:PEER_CHUNKS]
    ps = scratch[PEER_CHUNKS:2 * PEER_CHUNKS]
    acc_ref = scratch[2 * PEER_CHUNKS]
    ci = pl.program_id(1)
    ec = u_ref.shape[0] // PEER_CHUNKS
    n_il = ec // P_KEYS

    @pl.when(ci == 0)
    def _init():
        acc_ref[...] = jnp.zeros_like(acc_ref)

    xn = xn_ref[...]
    for j in range(PEER_CHUNKS):
        acts[j][...] = lax.dot_general(u_ref[j * ec:(j + 1) * ec, :], xn, _NT,
                                       preferred_element_type=F32)
    for j in range(PEER_CHUNKS):
        _routing_chunk(ec_ref, cf_ref, e2_ref, acts[j], ps[j], j * n_il, n_il)
        acc_ref[...] += jnp.dot(vt_ref[:, j * ec:(j + 1) * ec], ps[j][...],
                                preferred_element_type=F32)

    @pl.when(ci == pl.num_programs(1) - 1)
    def _finish():
        y = h1_ref[...] + acc_ref[...].T
        out_ref[...] = _rms(y, gf_ref[...])


def _peer_dense(xn2, ecut, cf, e2, u, vt, h1, gf, tt):
    T, D = xn2.shape
    E = u.shape[0]
    R = P_HEADS * P_KEYS
    n_steps = ecut.shape[1]
    step = E // n_steps
    ec = step // PEER_CHUNKS
    fsp = pl.BlockSpec((None, None, R // n_steps, tt), lambda t, k: (t, k, 0, 0))
    ssp = pl.BlockSpec((None, R, tt), lambda t, k: (t, 0, 0))
    tok = pl.BlockSpec((tt, D), lambda t, k: (t, 0))
    return pl.pallas_call(
        _peer_dense_kernel,
        grid=(T // tt, n_steps),
        in_specs=[tok, fsp, fsp, ssp,
                  pl.BlockSpec((step, D), lambda t, k: (k, 0)),
                  pl.BlockSpec((D, step), lambda t, k: (0, k)),
                  pl.BlockSpec((tt, D), lambda t, k: (t, 0), pipeline_mode=pl.Buffered(1)),
                  pl.BlockSpec((1, D), lambda t, k: (0, 0))],
        out_specs=tok,
        out_shape=jax.ShapeDtypeStruct((T, D), F32),
        scratch_shapes=([pltpu.VMEM((ec, tt), F32)] * PEER_CHUNKS
                        + [pltpu.VMEM((ec, tt), BF16)] * PEER_CHUNKS
                        + [pltpu.VMEM((D, tt), F32)]),
        compiler_params=_cparams(2),
        name="peer_dense",
    )(xn2, ecut, cf, e2, u, vt, h1, gf.reshape(1, D))


def _pad_cols(w, width):
    return jnp.pad(w, ((0, 0), (0, width - w.shape[1])))


def _segment_weights(w1):
    d, G = N_HEAD_DIM, N_KV_GROUPS
    half = CMP_BLOCK // 2
    w = w1.reshape(2, half, d, d)
    eye = jnp.eye(G, dtype=w1.dtype)
    w = w[:, :, None, :, None, :] * eye[None, None, :, None, :, None]
    return w.reshape(2, half * G * d, G * d)


def _segment_pos(pos):
    d, G = N_HEAD_DIM, N_KV_GROUPS
    half = CMP_BLOCK // 2
    p = jnp.broadcast_to(pos.reshape(2, half, 1, d), (2, half, G, d))
    return p.reshape(2, half * G * d)


def _values_t(v, tile):
    B, S, _ = v.shape
    v = v.reshape(B, S // tile, tile, N_KV_GROUPS, N_HEAD_DIM)
    return v.transpose(0, 3, 1, 4, 2)


def _forward(x, ln_mix_g, w_in, m_conv_w, m_conv_b, m_i_bias, m_f_bias, m_norm_g, w_m_out, cmp_k_pos, cmp_k_w1, cmp_k_w2, cmp_v_pos, cmp_v_w1, cmp_v_w2, w_n_out, w_out, ln_ffn_g, peer_wq, peer_k1, peer_k2, peer_u, peer_v, ln_f_g):
    B, S, D = x.shape
    T = B * S
    assert ln_mix_g.shape[0] == 1, "the final norm is fused into the last stage of a single layer"
    assert S % Q_BLOCK == 0 and S >= WINDOW + Q_BLOCK and S // SLC_BLOCK >= SLC_TOP
    l = 0
    d, G, H = N_HEAD_DIM, N_KV_GROUPS, M_HEADS
    MW = D
    KVW = G * d
    h = x.reshape(T, D)

    inv = ROPE_THETA ** (-jnp.arange(0, d, 2, dtype=F32) / d)
    ang = jnp.arange(S, dtype=F32)[:, None] * inv[None, :]
    ang = jnp.concatenate([ang, ang], axis=-1)
    cos = jnp.cos(ang)
    sin_signed = jnp.sin(ang) * jnp.concatenate([-jnp.ones((d // 2,), F32), jnp.ones((d // 2,), F32)])
    ncp = S // CMP_STRIDE
    n_slc = S // SLC_BLOCK
    n_idx = jnp.arange(ncp)[None, :] * CMP_STRIDE
    j_idx = jnp.arange(n_slc)[:, None] * SLC_BLOCK
    overlap_t = ((n_idx < j_idx + SLC_BLOCK) & (n_idx + CMP_BLOCK > j_idx)).astype(F32)
    tk = min(1024, S)
    expand_t = (jnp.arange(S).reshape(S // tk, tk, 1) // SLC_BLOCK
                == jnp.arange(n_slc)[None, None, :]).astype(BF16)

    wl = w_in[l]
    offs = [0]
    for wdt in (2 * MW, MW, MW, H, H, D, KVW, KVW, KVW, KVW, KVW, KVW, 3 * N_Q_HEADS, D, D):
        offs.append(offs[-1] + wdt)
    col = lambda i: wl[:, offs[i]:offs[i + 1]]
    w_small = _pad_cols(jnp.concatenate([col(3), col(4)], axis=1), LANES)
    w_ng = col(12).reshape(D, G, 3 * N_HPG)
    w_ng = jnp.pad(w_ng, ((0, 0), (0, 0), (0, LANES - 3 * N_HPG))).reshape(D, G * LANES)
    groups = [(col(0), F32), (col(1), BF16), (col(2), F32), (col(5), F32),
              (col(6), F32), (col(7), F32), (col(8), F32), (col(9), BF16),
              (col(10), F32), (col(11), BF16), (col(13), F32), (col(14), F32),
              (w_small, F32), (w_ng, F32)]
    w_cat = jnp.concatenate([g[0] for g in groups], axis=1).astype(BF16)
    (m_qk, m_v, m_o, n_q, n_kc, n_vc, n_ks, n_vs, n_kw, n_vw, g_a, g_b, small, n_g) = _in_proj(
        h, ln_mix_g[l], w_cat, [g[0].shape[1] for g in groups], [g[1] for g in groups], tm=256)

    small3 = small.reshape(B, S, LANES)
    grow = small3[:, :, :8].reshape(B, S // M_CHUNK, M_CHUNK, 8).transpose(0, 1, 3, 2)
    bias = jnp.concatenate([m_i_bias[l], m_f_bias[l]])
    hm = _mlstm(m_qk.reshape(B, S, 2 * MW), m_v.reshape(B, S, MW), m_o.reshape(B, S, MW),
                small3, grow, _pad_cols(bias.reshape(1, 2 * H), LANES), bias.reshape(2 * H, 1),
                m_conv_w[l], m_conv_b[l].reshape(1, 2 * MW), m_norm_g[l].reshape(1, MW))

    q_s, q_r, ks_r, kw_r = _nsa_prep(n_q.reshape(B, S, D), n_ks.reshape(B, S, KVW),
                                     n_kw.reshape(B, S, KVW), cos, sin_signed, ts=min(512, S))
    seg = CMP_STRIDE * KVW
    kcmp = _compress(n_kc.reshape(B, ncp, seg), _segment_pos(cmp_k_pos[l]),
                     _segment_weights(cmp_k_w1[l]).astype(BF16), cmp_k_w2[l].astype(BF16))
    vcmp = _compress(n_vc.reshape(B, ncp, seg), _segment_pos(cmp_v_pos[l]),
                     _segment_weights(cmp_v_w1[l]).astype(BF16), cmp_v_w2[l].astype(BF16))
    vcmp_t = vcmp.reshape(B, ncp, G, d).transpose(0, 2, 3, 1)
    on = _nsa_attn(q_s, q_r, kcmp, vcmp_t, ks_r, _values_t(n_vs.reshape(B, S, KVW), tk), kw_r,
                   _values_t(n_vw.reshape(B, S, KVW), Q_BLOCK), n_g.reshape(B, S, G * LANES),
                   overlap_t, expand_t)

    h1, xn2 = _mix(hm.reshape(T, MW), on.reshape(T, D), g_a, g_b, h,
                   w_m_out[l].astype(BF16), w_n_out[l].astype(BF16), w_out[l].astype(BF16),
                   ln_ffn_g[l], tm=256)

    tt = min(512, T)
    k1p = jnp.pad(peer_k1[l], ((0, 0), (0, P_HALF))).astype(BF16)
    k2p = jnp.pad(peer_k2[l], ((0, 0), (P_HALF, 0))).astype(BF16)
    ec, cf, e2 = _peer_score(xn2, peer_wq[l].astype(BF16), k1p, k2p, tt, PEER_STEPS)
    out = _peer_dense(xn2, ec, cf, e2, peer_u[l].astype(BF16), peer_v[l].T.astype(BF16),
                      h1, ln_f_g, tt)
    aux = dict(m_qk=m_qk, m_v=m_v, small=small, hm=hm, q_s=q_s, q_r=q_r, ks_r=ks_r, kw_r=kw_r,
               kcmp=kcmp, vcmp=vcmp, on=on, h1=h1, xn2=xn2, ec=ec, cf=cf, e2=e2)
    return out.reshape(B, S, D), aux


def kernel(x, ln_mix_g, w_in, m_conv_w, m_conv_b, m_i_bias, m_f_bias, m_norm_g, w_m_out, cmp_k_pos, cmp_k_w1, cmp_k_w2, cmp_v_pos, cmp_v_w1, cmp_v_w2, w_n_out, w_out, ln_ffn_g, peer_wq, peer_k1, peer_k2, peer_u, peer_v, ln_f_g):
    return _forward(x, ln_mix_g, w_in, m_conv_w, m_conv_b, m_i_bias, m_f_bias, m_norm_g, w_m_out,
                    cmp_k_pos, cmp_k_w1, cmp_k_w2, cmp_v_pos, cmp_v_w1, cmp_v_w2, w_n_out, w_out,
                    ln_ffn_g, peer_wq, peer_k1, peer_k2, peer_u, peer_v, ln_f_g)[0]
```

```python
import functools

import jax
import jax.numpy as jnp
from jax import lax
from jax.experimental import pallas as pl
from jax.experimental.pallas import tpu as pltpu

F32 = jnp.float32
BF16 = jnp.bfloat16

EPS = 1e-6
M_HEADS = 4
M_CONV = 4
M_CHUNK = 128
N_Q_HEADS = 8
N_KV_GROUPS = 2
N_HPG = N_Q_HEADS // N_KV_GROUPS
N_HEAD_DIM = 128
CMP_BLOCK = 32
CMP_STRIDE = 16
SLC_BLOCK = 64
SLC_TOP = 8
WINDOW = 512
Q_BLOCK = 128
ROPE_THETA = 10000.0
P_HEADS = 8
P_KEYS = 128
P_HALF = 64
P_TOPK = 16

LANES = 128
VMEM_LIMIT = 56 * 1024 * 1024
NEG = -1e30

_NT = (((1,), (1,)), ((), ()))
_TN = (((0,), (0,)), ((), ()))


def _cparams(n_axes, flags=None):
    return pltpu.CompilerParams(dimension_semantics=("arbitrary",) * n_axes,
                                vmem_limit_bytes=VMEM_LIMIT, flags=flags)


def _rms(x, g):
    return x * lax.rsqrt(jnp.mean(x * x, axis=-1, keepdims=True) + EPS) * g


def _gelu(x):
    return 0.5 * x * (1.0 + lax.erf(x * (2.0 ** -0.5)))


def _in_proj_kernel(x_ref, g_ref, w_ref, *o_refs, widths):
    xn = _rms(x_ref[...], g_ref[...]).astype(BF16)
    off = 0
    for o_ref, w in zip(o_refs, widths):
        o_ref[...] = jnp.dot(xn, w_ref[:, off:off + w],
                             preferred_element_type=F32).astype(o_ref.dtype)
        off += w


def _in_proj(x2d, g, w, widths, dtypes, tm):
    T, D = x2d.shape
    N = w.shape[1]
    return pl.pallas_call(
        functools.partial(_in_proj_kernel, widths=tuple(widths)),
        grid=(T // tm,),
        in_specs=[pl.BlockSpec((tm, D), lambda i: (i, 0)),
                  pl.BlockSpec((1, D), lambda i: (0, 0)),
                  pl.BlockSpec((D, N), lambda i: (0, 0), pipeline_mode=pl.Buffered(1))],
        out_specs=[pl.BlockSpec((tm, wd), lambda i: (i, 0)) for wd in widths],
        out_shape=[jax.ShapeDtypeStruct((T, wd), dt) for wd, dt in zip(widths, dtypes)],
        compiler_params=_cparams(1),
        name="in_proj",
    )(x2d, g.reshape(1, D), w)


def _log_sigmoid(x):
    return jnp.minimum(x, 0.0) - jnp.log1p(jnp.exp(-jnp.abs(x)))


def _mlstm_kernel(qk_ref, v_ref, og_ref, gcol_ref, grow_ref, brow_ref, bcol_ref, cw_ref, cb_ref,
                  ng_ref, out_ref, ext_ref, ct_ref, n_ref, m_ref, *, L, H, dh):
    W = H * dh

    @pl.when(pl.program_id(1) == 0)
    def _init():
        ext_ref[0:8, :] = jnp.zeros((8, 2 * W), F32)
        ct_ref[...] = jnp.zeros_like(ct_ref)
        n_ref[...] = jnp.zeros_like(n_ref)
        m_ref[...] = jnp.zeros_like(m_ref)

    x = qk_ref[...]
    ext_ref[8:8 + L, :] = x
    pre = cb_ref[...] + cw_ref[0:1, :] * ext_ref[8 - (M_CONV - 1):8 - (M_CONV - 1) + L, :]
    for j in range(1, M_CONV):
        o = 8 - (M_CONV - 1) + j
        pre = pre + cw_ref[j:j + 1, :] * ext_ref[o:o + L, :]
    ext_ref[0:8, :] = x[L - 8:L, :]
    qk = pre * jax.nn.sigmoid(pre)

    gcol = gcol_ref[...] + brow_ref[...]
    grow = grow_ref[...] + bcol_ref[...]
    rr = lax.broadcasted_iota(jnp.int32, (L, L), 0)
    cc = lax.broadcasted_iota(jnp.int32, (L, L), 1)
    causal = cc <= rr
    lower = causal.astype(F32)

    for h in range(H):
        q = qk[:, h * dh:(h + 1) * dh]
        k = qk[:, W + h * dh:W + (h + 1) * dh] * (dh ** -0.5)
        v = v_ref[:, h * dh:(h + 1) * dh].astype(F32)
        li_c = gcol[:, h:h + 1]
        lf_c = _log_sigmoid(gcol[:, H + h:H + h + 1])
        li_r = grow[h:h + 1, :]
        lf_r = _log_sigmoid(grow[H + h:H + h + 1, :])
        b_c = jnp.sum(lower * lf_r, axis=1, keepdims=True)
        b_r = jnp.sum((rr <= cc).astype(F32) * lf_c, axis=0, keepdims=True)
        m_prev = m_ref[h:h + 1, 0:1]

        dm = jnp.where(causal, b_c - b_r + li_r, -jnp.inf)
        inter = b_c + m_prev
        mt = jnp.maximum(inter, jnp.max(dm, axis=1, keepdims=True))
        dw = jnp.exp(dm - mt)
        iw = jnp.exp(inter - mt)
        qb = q.astype(BF16)
        kb = k.astype(BF16)
        wm = lax.dot_general(qb, kb, _NT, preferred_element_type=F32) * dw
        ct = ct_ref[h]
        n_row = n_ref[h:h + 1, :]
        num = (iw * jnp.dot(qb, ct.astype(BF16), preferred_element_type=F32)
               + jnp.dot(wm.astype(BF16), v.astype(BF16), preferred_element_type=F32))
        den = iw * jnp.sum(q * n_row, axis=1, keepdims=True) + jnp.sum(wm, axis=1, keepdims=True)
        hh = num / jnp.maximum(jnp.abs(den), jnp.exp(-mt))

        b_last = b_c[L - 1:L, :]
        g_r = b_last - b_r + li_r
        g_c = b_last - b_c + li_c
        m_new = jnp.maximum(b_last + m_prev, jnp.max(g_r, axis=1, keepdims=True))
        a = jnp.exp(b_last + m_prev - m_new)
        w_c = jnp.exp(g_c - m_new)
        ct_ref[h] = a * ct + lax.dot_general(kb, (v * w_c).astype(BF16), _TN,
                                             preferred_element_type=F32)
        n_ref[h:h + 1, :] = a * n_row + jnp.sum(k * w_c, axis=0, keepdims=True)
        m_ref[h:h + 1, :] = jnp.broadcast_to(m_new, (1, LANES))

        hn = hh * lax.rsqrt(jnp.mean(hh * hh, axis=1, keepdims=True) + EPS)
        o = hn * ng_ref[:, h * dh:(h + 1) * dh] * jax.nn.sigmoid(og_ref[:, h * dh:(h + 1) * dh])
        out_ref[:, h * dh:(h + 1) * dh] = o.astype(out_ref.dtype)


def _mlstm(m_qk, m_v, m_o, small, grow, brow, bcol, conv_w, conv_b, norm_g):
    B, S, W2 = m_qk.shape
    W = W2 // 2
    H, L = M_HEADS, M_CHUNK
    dh = W // H
    nc = S // L
    return pl.pallas_call(
        functools.partial(_mlstm_kernel, L=L, H=H, dh=dh),
        grid=(B, nc),
        in_specs=[pl.BlockSpec((None, L, W2), lambda b, c: (b, c, 0)),
                  pl.BlockSpec((None, L, W), lambda b, c: (b, c, 0)),
                  pl.BlockSpec((None, L, W), lambda b, c: (b, c, 0)),
                  pl.BlockSpec((None, L, LANES), lambda b, c: (b, c, 0)),
                  pl.BlockSpec((None, None, 8, L), lambda b, c: (b, c, 0, 0)),
                  pl.BlockSpec((1, LANES), lambda b, c: (0, 0)),
                  pl.BlockSpec((8, 1), lambda b, c: (0, 0)),
                  pl.BlockSpec((M_CONV, W2), lambda b, c: (0, 0)),
                  pl.BlockSpec((1, W2), lambda b, c: (0, 0)),
                  pl.BlockSpec((1, W), lambda b, c: (0, 0))],
        out_specs=pl.BlockSpec((None, L, W), lambda b, c: (b, c, 0)),
        out_shape=jax.ShapeDtypeStruct((B, S, W), BF16),
        scratch_shapes=[pltpu.VMEM((8 + L, W2), F32),
                        pltpu.VMEM((H, dh, dh), F32),
                        pltpu.VMEM((8, dh), F32),
                        pltpu.VMEM((8, LANES), F32)],
        compiler_params=_cparams(2),
        name="mlstm",
    )(m_qk, m_v, m_o, small, grow, brow, bcol, conv_w, conv_b, norm_g)


def _rope(x, cos, sin_signed):
    return x * cos + pltpu.roll(x, N_HEAD_DIM // 2, axis=1) * sin_signed


def _nsa_prep_kernel(q_ref, ks_ref, kw_ref, cos_ref, sin_ref, qs_out, qr_out, ks_out, kw_out):
    cos = cos_ref[...]
    sin = sin_ref[...]
    scale = N_HEAD_DIM ** -0.5
    d = N_HEAD_DIM
    for h in range(N_Q_HEADS):
        x = q_ref[:, h * d:(h + 1) * d] * scale
        qs_out[:, h * d:(h + 1) * d] = x.astype(BF16)
        qr_out[:, h * d:(h + 1) * d] = _rope(x, cos, sin).astype(BF16)
    for g in range(N_KV_GROUPS):
        ks_out[:, g * d:(g + 1) * d] = _rope(ks_ref[:, g * d:(g + 1) * d], cos, sin).astype(BF16)
        kw_out[:, g * d:(g + 1) * d] = _rope(kw_ref[:, g * d:(g + 1) * d], cos, sin).astype(BF16)


def _nsa_prep(n_q, n_ks, n_kw, cos, sin_signed, ts):
    B, S, NW = n_q.shape
    KW = n_ks.shape[-1]
    tok = lambda w: pl.BlockSpec((None, ts, w), lambda b, i: (b, i, 0))
    tab = pl.BlockSpec((ts, N_HEAD_DIM), lambda b, i: (i, 0))
    return pl.pallas_call(
        _nsa_prep_kernel,
        grid=(B, S // ts),
        in_specs=[tok(NW), tok(KW), tok(KW), tab, tab],
        out_specs=[tok(NW), tok(NW), tok(KW), tok(KW)],
        out_shape=[jax.ShapeDtypeStruct((B, S, NW), BF16), jax.ShapeDtypeStruct((B, S, NW), BF16),
                   jax.ShapeDtypeStruct((B, S, KW), BF16), jax.ShapeDtypeStruct((B, S, KW), BF16)],
        compiler_params=_cparams(2),
        name="nsa_prep",
    )(n_q, n_ks, n_kw, cos, sin_signed)


def _compress_kernel(x_ref, p_ref, wh_ref, w2_ref, o_ref):
    x = x_ref[...]
    ns = x.shape[0]
    y0 = jnp.dot((x + p_ref[0:1, :]).astype(BF16), wh_ref[0], preferred_element_type=F32)
    y1 = jnp.dot((x + p_ref[1:2, :]).astype(BF16), wh_ref[1], preferred_element_type=F32)
    act = _gelu(y0 + pltpu.roll(y1, ns - 1, axis=0))
    d = N_HEAD_DIM
    for g in range(N_KV_GROUPS):
        o_ref[:, g * d:(g + 1) * d] = jnp.dot(act[:, g * d:(g + 1) * d].astype(BF16), w2_ref[...],
                                              preferred_element_type=F32).astype(o_ref.dtype)


def _compress(x_seg, pos2, wh, w2):
    B, NS, SW = x_seg.shape
    GW = N_KV_GROUPS * N_HEAD_DIM
    return pl.pallas_call(
        _compress_kernel,
        grid=(B,),
        in_specs=[pl.BlockSpec((None, NS, SW), lambda b: (b, 0, 0)),
                  pl.BlockSpec((2, SW), lambda b: (0, 0)),
                  pl.BlockSpec((2, SW, GW), lambda b: (0, 0, 0)),
                  pl.BlockSpec((N_HEAD_DIM, N_HEAD_DIM), lambda b: (0, 0))],
        out_specs=pl.BlockSpec((None, NS, GW), lambda b: (b, 0, 0)),
        out_shape=jax.ShapeDtypeStruct((B, NS, GW), BF16),
        compiler_params=_cparams(1),
        name="nsa_compress",
    )(x_seg, pos2, wh, w2)


def _nsa_attn_kernel(qs_ref, qr_ref, kc_ref, vct_ref, ks_ref, vst_ref, kw_ref, vwt_ref, gate_ref,
                     ovt_ref, ext_ref, o_ref, acc_ref, *, TQ, TK):
    i = pl.program_id(2)
    t0 = i * TQ
    d = N_HEAD_DIM
    NC = kc_ref.shape[0]
    J = ovt_ref.shape[0]
    NQ = N_HPG * TQ
    cols = [slice(hh * TQ, (hh + 1) * TQ) for hh in range(N_HPG)]

    def stack_heads(ref):
        return jnp.concatenate([ref[:, hh * d:(hh + 1) * d] for hh in range(N_HPG)], axis=0)

    s = lax.dot_general(kc_ref[...], stack_heads(qs_ref), _NT, preferred_element_type=F32)
    nrow = lax.broadcasted_iota(jnp.int32, (NC, TQ), 0)
    tcol = t0 + lax.broadcasted_iota(jnp.int32, (NC, TQ), 1)
    cmask = (nrow * CMP_STRIDE + (CMP_BLOCK - 1)) <= tcol
    psum = jnp.zeros((NC, TQ), F32)
    ps = []
    for hh in range(N_HPG):
        sh = jnp.where(cmask, s[:, cols[hh]], NEG)
        p = jnp.where(cmask, jnp.exp(sh - jnp.max(sh, axis=0, keepdims=True)), 0.0)
        l = jnp.sum(p, axis=0, keepdims=True)
        p = p * (1.0 / jnp.where(l > 0.0, l, 1.0))
        psum = psum + p
        ps.append(p.astype(BF16))
    o_cmp = jnp.dot(vct_ref[...], jnp.concatenate(ps, axis=1), preferred_element_type=F32)

    imp = jnp.dot(ovt_ref[...], psum, preferred_element_type=F32, precision=lax.Precision.HIGHEST)
    jrow = lax.broadcasted_iota(jnp.int32, (J, TQ), 0)
    tj = t0 + lax.broadcasted_iota(jnp.int32, (J, TQ), 1)
    cur = jnp.right_shift(tj, SLC_BLOCK.bit_length() - 1)
    forced = (jrow == 0) | (jrow == cur) | (jrow == cur - 1)
    valid = (jrow * SLC_BLOCK) <= tj
    score = jnp.where(forced, 1e9, jnp.where(valid, imp, -1e9))
    sel = jnp.zeros((J, TQ), F32)
    for _ in range(SLC_TOP):
        mx = jnp.max(score, axis=0, keepdims=True)
        idx = jnp.min(jnp.where(score == mx, jrow, 1 << 30), axis=0, keepdims=True)
        hit = jrow == idx
        sel = jnp.where(hit & (mx > -1e8), 1.0, sel)
        score = jnp.where(hit, -3e38, score)
    selb = sel.astype(BF16)

    qr = stack_heads(qr_ref)
    acc_ref[...] = jnp.zeros(acc_ref.shape, F32)
    krow = lax.broadcasted_iota(jnp.int32, (TK, TQ), 0)
    tk_col = t0 + lax.broadcasted_iota(jnp.int32, (TK, TQ), 1)

    def sweep(kt, carry):
        m, l = carry
        k0 = pl.multiple_of(kt * TK, TK)
        st = lax.dot_general(ks_ref[pl.ds(k0, TK), :], qr, _NT, preferred_element_type=F32)
        bm = jnp.dot(ext_ref[kt], selb, preferred_element_type=F32)
        mask = (bm > 0.5) & ((k0 + krow) <= tk_col)
        ms, ls, als, pts = [], [], [], []
        for hh in range(N_HPG):
            sh = jnp.where(mask, st[:, cols[hh]], NEG)
            m_prev = m[:, cols[hh]]
            m_new = jnp.maximum(m_prev, jnp.max(sh, axis=0, keepdims=True))
            p = jnp.exp(sh - m_new)
            al = jnp.exp(m_prev - m_new)
            ms.append(m_new)
            als.append(al)
            ls.append(al * l[:, cols[hh]] + jnp.sum(p, axis=0, keepdims=True))
            pts.append(p.astype(BF16))
        acc_ref[...] = (acc_ref[...] * jnp.concatenate(als, axis=1)
                        + jnp.dot(vst_ref[kt], jnp.concatenate(pts, axis=1),
                                  preferred_element_type=F32))
        return jnp.concatenate(ms, axis=1), jnp.concatenate(ls, axis=1)

    m0 = jnp.full((1, NQ), NEG, F32)
    l0 = jnp.zeros((1, NQ), F32)
    _, l_s = lax.fori_loop(0, (t0 + TQ + TK - 1) // TK, sweep, (m0, l0))
    o_slc = acc_ref[...] * (1.0 / jnp.where(l_s > 0.0, l_s, 1.0))

    span = WINDOW + TQ
    wt0 = jnp.maximum(i - WINDOW // TQ, 0)
    ws = pl.multiple_of(wt0 * TQ, TQ)
    sw = lax.dot_general(kw_ref[pl.ds(ws, span), :], qr, _NT, preferred_element_type=F32)
    kp = ws + lax.broadcasted_iota(jnp.int32, (span, TQ), 0)
    tq_w = t0 + lax.broadcasted_iota(jnp.int32, (span, TQ), 1)
    wmask = (kp <= tq_w) & (kp > tq_w - WINDOW)
    pws, lws = [], []
    for hh in range(N_HPG):
        sh = jnp.where(wmask, sw[:, cols[hh]], NEG)
        p = jnp.exp(sh - jnp.max(sh, axis=0, keepdims=True))
        lws.append(jnp.sum(p, axis=0, keepdims=True))
        pws.append(p.astype(BF16))
    pw = jnp.concatenate(pws, axis=1)
    o_win = jnp.dot(vwt_ref[wt0], pw[0:TQ, :], preferred_element_type=F32)
    for j in range(1, span // TQ):
        o_win = o_win + jnp.dot(vwt_ref[wt0 + j], pw[j * TQ:(j + 1) * TQ, :],
                                preferred_element_type=F32)
    o_win = o_win * (1.0 / jnp.concatenate(lws, axis=1))

    gt = jax.nn.sigmoid(gate_ref[...]).T
    for hh in range(N_HPG):
        o = (gt[3 * hh:3 * hh + 1, :] * o_cmp[:, cols[hh]]
             + gt[3 * hh + 1:3 * hh + 2, :] * o_slc[:, cols[hh]]
             + gt[3 * hh + 2:3 * hh + 3, :] * o_win[:, cols[hh]])
        o_ref[:, hh * d:(hh + 1) * d] = o.T.astype(o_ref.dtype)


def _nsa_attn(q_s, q_r, kcmp, vcmp_t, ks_r, vs_t, kw_r, vw_t, gates, overlap_t, expand_t):
    B, S, _ = q_s.shape
    G, d, TQ = N_KV_GROUPS, N_HEAD_DIM, Q_BLOCK
    NC = kcmp.shape[1]
    TK = vs_t.shape[-1]
    gw = N_HPG * d
    qspec = pl.BlockSpec((None, TQ, gw), lambda b, g, i: (b, i, g))
    kspec = pl.BlockSpec((None, S, d), lambda b, g, i: (b, 0, g))
    tspec = lambda a: pl.BlockSpec((None, None) + a.shape[2:],
                                   lambda b, g, i: (b, g) + (0,) * (a.ndim - 2))
    const = lambda a: pl.BlockSpec(a.shape, lambda b, g, i: (0,) * a.ndim)
    return pl.pallas_call(
        functools.partial(_nsa_attn_kernel, TQ=TQ, TK=TK),
        grid=(B, G, S // TQ),
        in_specs=[qspec, qspec,
                  pl.BlockSpec((None, NC, d), lambda b, g, i: (b, 0, g)), tspec(vcmp_t),
                  kspec, tspec(vs_t), kspec, tspec(vw_t),
                  pl.BlockSpec((None, TQ, LANES), lambda b, g, i: (b, i, g)),
                  const(overlap_t), const(expand_t)],
        out_specs=qspec,
        out_shape=jax.ShapeDtypeStruct((B, S, G * gw), BF16),
        scratch_shapes=[pltpu.VMEM((d, N_HPG * TQ), F32)],
        compiler_params=_cparams(3),
        name="nsa_attn",
    )(q_s, q_r, kcmp, vcmp_t, ks_r, vs_t, kw_r, vw_t, gates, overlap_t, expand_t)


def _mix_kernel(hm_ref, on_ref, ga_ref, gb_ref, x_ref, wm_ref, wn_ref, wo_ref, g2_ref,
                h1_ref, xn2_ref):
    ya = jnp.dot(hm_ref[...], wm_ref[...], preferred_element_type=F32)
    yb = jnp.dot(on_ref[...], wn_ref[...], preferred_element_type=F32)
    mix = jax.nn.sigmoid(ga_ref[...]) * ya + jax.nn.sigmoid(gb_ref[...]) * yb
    h1 = x_ref[...] + jnp.dot(mix.astype(BF16), wo_ref[...], preferred_element_type=F32)
    h1_ref[...] = h1
    xn2_ref[...] = _rms(h1, g2_ref[...]).astype(BF16)


def _mix(hm, on, g_a, g_b, x2d, w_m, w_n, w_o, g2, tm):
    T, D = x2d.shape
    tok = pl.BlockSpec((tm, D), lambda i: (i, 0))
    wsp = pl.BlockSpec((D, D), lambda i: (0, 0))
    return pl.pallas_call(
        _mix_kernel,
        grid=(T // tm,),
        in_specs=[tok, tok, tok, tok, tok, wsp, wsp, wsp, pl.BlockSpec((1, D), lambda i: (0, 0))],
        out_specs=[tok, tok],
        out_shape=[jax.ShapeDtypeStruct((T, D), F32), jax.ShapeDtypeStruct((T, D), BF16)],
        compiler_params=_cparams(1),
        name="mix",
    )(hm, on, g_a, g_b, x2d, w_m, w_n, w_o, g2.reshape(1, D))


SUBLANES = 8


def _batcher_pairs(n):
    pairs = []

    def merge(lo, hi, r):
        step = 2 * r
        if step < hi - lo:
            merge(lo, hi, step)
            merge(lo + r, hi, step)
            pairs.extend((i, i + r) for i in range(lo + r, hi - r, step))
        else:
            pairs.append((lo, lo + r))

    def sort(lo, hi):
        if hi - lo >= 1:
            mid = lo + (hi - lo) // 2
            sort(lo, mid)
            sort(mid + 1, hi)
            merge(lo, hi, 1)

    sort(0, n - 1)
    return pairs


def _compare_exchange(xs, i, j):
    xs[i], xs[j] = jnp.maximum(xs[i], xs[j]), jnp.minimum(xs[i], xs[j])


def _sorted_top(xs, n_real=None):
    k = len(xs)
    n_real = k if n_real is None else n_real
    xs = list(xs)
    for i, j in _batcher_pairs(k):
        if j < n_real:
            _compare_exchange(xs, i, j)
    shift = SUBLANES // 2
    while shift:
        ys = [pltpu.roll(x, shift, axis=0) for x in xs]
        xs = [jnp.maximum(xs[i], ys[k - 1 - i]) for i in range(k)]
        d = k // 2
        while d:
            for i in range(k):
                if not i & d:
                    _compare_exchange(xs, i, i + d)
            d //= 2
        shift //= 2
    return xs


def _prefix_last(test, vals):
    p8 = vals[7]
    c8 = test(p8)
    p4 = jnp.where(c8, vals[11], vals[3])
    c4 = test(p4)
    p2 = jnp.where(c8, jnp.where(c4, vals[13], vals[9]), jnp.where(c4, vals[5], vals[1]))
    c2 = test(p2)
    hi = jnp.where(c4, jnp.where(c2, vals[14], vals[12]), jnp.where(c2, vals[10], vals[8]))
    lo = jnp.where(c4, jnp.where(c2, vals[6], vals[4]), jnp.where(c2, vals[2], vals[0]))
    p1 = jnp.where(c8, hi, lo)
    c1 = test(p1)
    best = jnp.where(c8, p8, jnp.inf)
    best = jnp.where(c4, p4, best)
    best = jnp.where(c2, p2, best)
    best = jnp.where(c1, p1, best)
    return jnp.where(test(vals[15]), vals[15], best)


def _peer_score_kernel(xn_ref, wq_ref, k1_ref, k2_ref, ec_ref, cf_ref, e2_ref):
    q = jnp.dot(xn_ref[...], wq_ref[...], preferred_element_type=F32)
    nk = P_KEYS
    n_steps = ec_ref.shape[0]
    per = nk // n_steps
    for h in range(P_HEADS):
        qh = q[:, h * 2 * P_HALF:(h + 1) * 2 * P_HALF].astype(BF16)
        s1 = lax.dot_general(k1_ref[...], qh, _NT, preferred_element_type=F32)
        s2 = lax.dot_general(k2_ref[...], qh, _NT, preferred_element_type=F32)
        groups = lambda s: [s[SUBLANES * g:SUBLANES * (g + 1), :] for g in range(nk // SUBLANES)]
        v1 = _sorted_top(groups(s1))
        v2 = _sorted_top(groups(s2))
        sub = lax.broadcasted_iota(jnp.int32, v1[0].shape, 0)
        lay = lambda vs: functools.reduce(lambda acc, r: jnp.where(sub == r, vs[r], acc),
                                          range(1, SUBLANES), vs[0])
        v2lo, v2hi, v1hi = lay(v2[:SUBLANES]), lay(v2[SUBLANES:]), lay(v1[SUBLANES:])
        cands = [v1[0] + v2lo, v1[0] + v2hi, v1[1] + v2lo]
        cands += [jnp.where(sub < P_TOPK // (a + 1), v1[a] + v2lo, -jnp.inf)
                  for a in range(2, SUBLANES)]
        cands.append(v1hi + v2[0])
        n_real = len(cands)
        cands += [jnp.full(sub.shape, -jnp.inf, F32)] * (P_TOPK - n_real)
        top = [t[0:1, :] for t in _sorted_top(cands, n_real)]
        v1r = [v[0:1, :] for v in v1]
        v2r = [v[0:1, :] for v in v2]
        z = jnp.exp(top[0] - top[0])
        for t in top[1:]:
            z = z + jnp.exp(t - top[0])
        cut = _prefix_last(lambda p: s1 + p >= top[-1], v2r)
        ecut = jnp.exp(cut - v2r[0])
        coef = jnp.exp(s1 - v1r[0]) / z
        for k in range(n_steps):
            ec_ref[k, h * per:(h + 1) * per, :] = ecut[k * per:(k + 1) * per, :]
            cf_ref[k, h * per:(h + 1) * per, :] = coef[k * per:(k + 1) * per, :]
        e2_ref[h * nk:(h + 1) * nk, :] = jnp.exp(s2 - v2r[0])


def _peer_score(xn2, wq, k1p, k2p, tt, n_steps):
    T, D = xn2.shape
    nt = T // tt
    R = P_HEADS * P_KEYS
    fsp = pl.BlockSpec((None, n_steps, R // n_steps, tt), lambda i: (i, 0, 0, 0))
    fsh = jax.ShapeDtypeStruct((nt, n_steps, R // n_steps, tt), F32)
    ssp = pl.BlockSpec((None, R, tt), lambda i: (i, 0, 0))
    ssh = jax.ShapeDtypeStruct((nt, R, tt), F32)
    return pl.pallas_call(
        _peer_score_kernel,
        grid=(nt,),
        in_specs=[pl.BlockSpec((tt, D), lambda i: (i, 0)),
                  pl.BlockSpec(wq.shape, lambda i: (0, 0)),
                  pl.BlockSpec(k1p.shape, lambda i: (0, 0)),
                  pl.BlockSpec(k2p.shape, lambda i: (0, 0))],
        out_specs=[fsp, fsp, ssp],
        out_shape=[fsh, fsh, ssh],
        compiler_params=_cparams(1),
        name="peer_score",
    )(xn2, wq, k1p, k2p)


PEER_ROWS = 16
PEER_COLS = 256
PEER_STEPS = 8
PEER_CHUNKS = 2


def _routing_chunk(ec_ref, cf_ref, e2_ref, act_ref, p_ref, il0, n_il):
    nk = P_KEYS
    tt = act_ref.shape[1]
    per = ec_ref.shape[0] // P_HEADS
    blk = (PEER_ROWS, PEER_COLS)
    for il in range(il0, il0 + n_il):
        for lb in range(tt // PEER_COLS):
            lanes = slice(lb * PEER_COLS, (lb + 1) * PEER_COLS)
            ec = [jnp.broadcast_to(ec_ref[h * per + il:h * per + il + 1, lanes], blk)
                  for h in range(P_HEADS)]
            cf = [jnp.broadcast_to(cf_ref[h * per + il:h * per + il + 1, lanes], blk)
                  for h in range(P_HEADS)]
            for rb in range(nk // PEER_ROWS):
                w = None
                for h in range(P_HEADS):
                    e2 = e2_ref[h * nk + rb * PEER_ROWS:h * nk + (rb + 1) * PEER_ROWS, lanes]
                    t = jnp.where(e2 >= ec[h], cf[h] * e2, 0.0)
                    w = t if w is None else w + t
                dst = slice((il - il0) * nk + rb * PEER_ROWS, (il - il0) * nk + (rb + 1) * PEER_ROWS)
                p_ref[dst, lanes] = (w * _gelu(act_ref[dst, lanes])).astype(BF16)


def _peer_dense_kernel(xn_ref, ec_ref, cf_ref, e2_ref, u_ref, vt_ref, h1_ref, gf_ref,
                       out_ref, *scratch):
    acts = scratch[:PEER_CHUNKS]
    ps = scratch[PEER_CHUNKS:2 * PEER_CHUNKS]
    acc_ref = scratch[2 * PEER_CHUNKS]
    ci = pl.program_id(1)
    ec = u_ref.shape[0] // PEER_CHUNKS
    n_il = ec // P_KEYS

    @pl.when(ci == 0)
    def _init():
        acc_ref[...] = jnp.zeros_like(acc_ref)

    xn = xn_ref[...]
    for j in range(PEER_CHUNKS):
        acts[j][...] = lax.dot_general(u_ref[j * ec:(j + 1) * ec, :], xn, _NT,
                                       preferred_element_type=F32)
    for j in range(PEER_CHUNKS):
        _routing_chunk(ec_ref, cf_ref, e2_ref, acts[j], ps[j], j * n_il, n_il)
        acc_ref[...] += jnp.dot(vt_ref[:, j * ec:(j + 1) * ec], ps[j][...],
                                preferred_element_type=F32)

    @pl.when(ci == pl.num_programs(1) - 1)
    def _finish():
        y = h1_ref[...] + acc_ref[...].T
        out_ref[...] = _rms(y, gf_ref[...])


def _peer_dense(xn2, ecut, cf, e2, u, vt, h1, gf, tt):
    T, D = xn2.shape
    E = u.shape[0]
    R = P_HEADS * P_KEYS
    n_steps = ecut.shape[1]
    step = E // n_steps
    ec = step // PEER_CHUNKS
    fsp = pl.BlockSpec((None, None, R // n_steps, tt), lambda t, k: (t, k, 0, 0))
    ssp = pl.BlockSpec((None, R, tt), lambda t, k: (t, 0, 0))
    tok = pl.BlockSpec((tt, D), lambda t, k: (t, 0))
    return pl.pallas_call(
        _peer_dense_kernel,
        grid=(T // tt, n_steps),
        in_specs=[tok, fsp, fsp, ssp,
                  pl.BlockSpec((step, D), lambda t, k: (k, 0)),
                  pl.BlockSpec((D, step), lambda t, k: (0, k)),
                  pl.BlockSpec((tt, D), lambda t, k: (t, 0), pipeline_mode=pl.Buffered(1)),
                  pl.BlockSpec((1, D), lambda t, k: (0, 0))],
        out_specs=tok,
        out_shape=jax.ShapeDtypeStruct((T, D), F32),
        scratch_shapes=([pltpu.VMEM((ec, tt), F32)] * PEER_CHUNKS
                        + [pltpu.VMEM((ec, tt), BF16)] * PEER_CHUNKS
                        + [pltpu.VMEM((D, tt), F32)]),
        compiler_params=_cparams(2),
        name="peer_dense",
    )(xn2, ecut, cf, e2, u, vt, h1, gf.reshape(1, D))


def _pad_cols(w, width):
    return jnp.pad(w, ((0, 0), (0, width - w.shape[1])))


def _segment_weights(w1):
    d, G = N_HEAD_DIM, N_KV_GROUPS
    half = CMP_BLOCK // 2
    w = w1.reshape(2, half, d, d)
    eye = jnp.eye(G, dtype=w1.dtype)
    w = w[:, :, None, :, None, :] * eye[None, None, :, None, :, None]
    return w.reshape(2, half * G * d, G * d)


def _segment_pos(pos):
    d, G = N_HEAD_DIM, N_KV_GROUPS
    half = CMP_BLOCK // 2
    p = jnp.broadcast_to(pos.reshape(2, half, 1, d), (2, half, G, d))
    return p.reshape(2, half * G * d)


def _values_t(v, tile):
    B, S, _ = v.shape
    v = v.reshape(B, S // tile, tile, N_KV_GROUPS, N_HEAD_DIM)
    return v.transpose(0, 3, 1, 4, 2)


def _forward(x, ln_mix_g, w_in, m_conv_w, m_conv_b, m_i_bias, m_f_bias, m_norm_g, w_m_out, cmp_k_pos, cmp_k_w1, cmp_k_w2, cmp_v_pos, cmp_v_w1, cmp_v_w2, w_n_out, w_out, ln_ffn_g, peer_wq, peer_k1, peer_k2, peer_u, peer_v, ln_f_g):
    B, S, D = x.shape
    T = B * S
    assert ln_mix_g.shape[0] == 1, "the final norm is fused into the last stage of a single layer"
    assert S % Q_BLOCK == 0 and S >= WINDOW + Q_BLOCK and S // SLC_BLOCK >= SLC_TOP
    l = 0
    d, G, H = N_HEAD_DIM, N_KV_GROUPS, M_HEADS
    MW = D
    KVW = G * d
    h = x.reshape(T, D)

    inv = ROPE_THETA ** (-jnp.arange(0, d, 2, dtype=F32) / d)
    ang = jnp.arange(S, dtype=F32)[:, None] * inv[None, :]
    ang = jnp.concatenate([ang, ang], axis=-1)
    cos = jnp.cos(ang)
    sin_signed = jnp.sin(ang) * jnp.concatenate([-jnp.ones((d // 2,), F32), jnp.ones((d // 2,), F32)])
    ncp = S // CMP_STRIDE
    n_slc = S // SLC_BLOCK
    n_idx = jnp.arange(ncp)[None, :] * CMP_STRIDE
    j_idx = jnp.arange(n_slc)[:, None] * SLC_BLOCK
    overlap_t = ((n_idx < j_idx + SLC_BLOCK) & (n_idx + CMP_BLOCK > j_idx)).astype(F32)
    tk = min(1024, S)
    expand_t = (jnp.arange(S).reshape(S // tk, tk, 1) // SLC_BLOCK
                == jnp.arange(n_slc)[None, None, :]).astype(BF16)

    wl = w_in[l]
    offs = [0]
    for wdt in (2 * MW, MW, MW, H, H, D, KVW, KVW, KVW, KVW, KVW, KVW, 3 * N_Q_HEADS, D, D):
        offs.append(offs[-1] + wdt)
    col = lambda i: wl[:, offs[i]:offs[i + 1]]
    w_small = _pad_cols(jnp.concatenate([col(3), col(4)], axis=1), LANES)
    w_ng = col(12).reshape(D, G, 3 * N_HPG)
    w_ng = jnp.pad(w_ng, ((0, 0), (0, 0), (0, LANES - 3 * N_HPG))).reshape(D, G * LANES)
    groups = [(col(0), F32), (col(1), BF16), (col(2), F32), (col(5), F32),
              (col(6), F32), (col(7), F32), (col(8), F32), (col(9), BF16),
              (col(10), F32), (col(11), BF16), (col(13), F32), (col(14), F32),
              (w_small, F32), (w_ng, F32)]
    w_cat = jnp.concatenate([g[0] for g in groups], axis=1).astype(BF16)
    (m_qk, m_v, m_o, n_q, n_kc, n_vc, n_ks, n_vs, n_kw, n_vw, g_a, g_b, small, n_g) = _in_proj(
        h, ln_mix_g[l], w_cat, [g[0].shape[1] for g in groups], [g[1] for g in groups], tm=256)

    small3 = small.reshape(B, S, LANES)
    grow = small3[:, :, :8].reshape(B, S // M_CHUNK, M_CHUNK, 8).transpose(0, 1, 3, 2)
    bias = jnp.concatenate([m_i_bias[l], m_f_bias[l]])
    hm = _mlstm(m_qk.reshape(B, S, 2 * MW), m_v.reshape(B, S, MW), m_o.reshape(B, S, MW),
                small3, grow, _pad_cols(bias.reshape(1, 2 * H), LANES), bias.reshape(2 * H, 1),
                m_conv_w[l], m_conv_b[l].reshape(1, 2 * MW), m_norm_g[l].reshape(1, MW))

    q_s, q_r, ks_r, kw_r = _nsa_prep(n_q.reshape(B, S, D), n_ks.reshape(B, S, KVW),
                                     n_kw.reshape(B, S, KVW), cos, sin_signed, ts=min(512, S))
    seg = CMP_STRIDE * KVW
    kcmp = _compress(n_kc.reshape(B, ncp, seg), _segment_pos(cmp_k_pos[l]),
                     _segment_weights(cmp_k_w1[l]).astype(BF16), cmp_k_w2[l].astype(BF16))
    vcmp = _compress(n_vc.reshape(B, ncp, seg), _segment_pos(cmp_v_pos[l]),
                     _segment_weights(cmp_v_w1[l]).astype(BF16), cmp_v_w2[l].astype(BF16))
    vcmp_t = vcmp.reshape(B, ncp, G, d).transpose(0, 2, 3, 1)
    on = _nsa_attn(q_s, q_r, kcmp, vcmp_t, ks_r, _values_t(n_vs.reshape(B, S, KVW), tk), kw_r,
                   _values_t(n_vw.reshape(B, S, KVW), Q_BLOCK), n_g.reshape(B, S, G * LANES),
                   overlap_t, expand_t)

    h1, xn2 = _mix(hm.reshape(T, MW), on.reshape(T, D), g_a, g_b, h,
                   w_m_out[l].astype(BF16), w_n_out[l].astype(BF16), w_out[l].astype(BF16),
                   ln_ffn_g[l], tm=256)

    tt = min(512, T)
    k1p = jnp.pad(peer_k1[l], ((0, 0), (0, P_HALF))).astype(BF16)
    k2p = jnp.pad(peer_k2[l], ((0, 0), (P_HALF, 0))).astype(BF16)
    ec, cf, e2 = _peer_score(xn2, peer_wq[l].astype(BF16), k1p, k2p, tt, PEER_STEPS)
    out = _peer_dense(xn2, ec, cf, e2, peer_u[l].astype(BF16), peer_v[l].T.astype(BF16),
                      h1, ln_f_g, tt)
    aux = dict(m_qk=m_qk, m_v=m_v, small=small, hm=hm, q_s=q_s, q_r=q_r, ks_r=ks_r, kw_r=kw_r,
               kcmp=kcmp, vcmp=vcmp, on=on, h1=h1, xn2=xn2, ec=ec, cf=cf, e2=e2)
    return out.reshape(B, S, D), aux


def kernel(x, ln_mix_g, w_in, m_conv_w, m_conv_b, m_i_bias, m_f_bias, m_norm_g, w_m_out, cmp_k_pos, cmp_k_w1, cmp_k_w2, cmp_v_pos, cmp_v_w1, cmp_v_w2, w_n_out, w_out, ln_ffn_g, peer_wq, peer_k1, peer_k2, peer_u, peer_v, ln_f_g):
    return _forward(x, ln_mix_g, w_in, m_conv_w, m_conv_b, m_i_bias, m_f_bias, m_norm_g, w_m_out,
                    cmp_k_pos, cmp_k_w1, cmp_k_w2, cmp_v_pos, cmp_v_w1, cmp_v_w2, w_n_out, w_out,
                    ln_ffn_g, peer_wq, peer_k1, peer_k2, peer_u, peer_v, ln_f_g)[0]
```
